```python
import math
import jax, jax.numpy as jnp
from jax import lax
import numpy as np

D_MODEL = 1024
BATCH = 16
SEQ = 4096
DEPTH = 1
DEC_BATCH = 2
DEC_SEQ = 16384
PAST_LEN = 128

ATTN_WIDTH = D_MODEL // 2
CONV_WIDTH = D_MODEL - ATTN_WIDTH
N_HEADS = 4
HEAD_DIM = ATTN_WIDTH // N_HEADS // 2
V_DIM = 2 * HEAD_DIM
QK_WIDTH = N_HEADS * 2 * HEAD_DIM
IN_WIDTH = 2 * QK_WIDTH + ATTN_WIDTH + 2 * CONV_WIDTH
ROT_DIM = HEAD_DIM // 4
ROPE_THETA = 500000.0
CONV_KERNEL = 31
CONV_PAD = CONV_KERNEL // 2
N_GROUPS = 4
EXPERTS_PER_GROUP = 8
N_EXPERTS = N_GROUPS * EXPERTS_PER_GROUP
TOP_K = 2
EXPERT_FF = 512
Q_BLOCK = 128
MOE_BLOCK = 128
EPS = 1e-6

kernel_name = 'hymba_diffattn_conformer_hmoe_encoder'


def rms_norm(x, g):
    xf = x.astype(jnp.float32)
    y = xf * lax.rsqrt(jnp.mean(xf * xf, axis=-1, keepdims=True) + EPS)
    return (y * g.astype(jnp.float32)).astype(x.dtype)


def layer_norm(x, g, b):
    xf = x.astype(jnp.float32)
    mu = jnp.mean(xf, axis=-1, keepdims=True)
    var = jnp.mean(jnp.square(xf - mu), axis=-1, keepdims=True)
    y = (xf - mu) * lax.rsqrt(var + EPS)
    return (y * g.astype(jnp.float32) + b.astype(jnp.float32)).astype(x.dtype)


def modulate(h, shift, scale):
    return h * (1.0 + scale[:, None, :]) + shift[:, None, :]


def apply_rotary(x, pos):
    inv_freq = ROPE_THETA ** (-jnp.arange(0, ROT_DIM, 2, dtype=jnp.float32) / ROT_DIM)
    ang = pos[:, None] * inv_freq[None, :]
    cos = jnp.cos(ang)[:, None, None, :]
    sin = jnp.sin(ang)[:, None, None, :]
    xr = x[..., :ROT_DIM].astype(jnp.float32)
    x1 = xr[..., :ROT_DIM // 2]
    x2 = xr[..., ROT_DIM // 2:]
    rot = jnp.concatenate([x1 * cos - x2 * sin, x2 * cos + x1 * sin], axis=-1).astype(x.dtype)
    return jnp.concatenate([rot, x[..., ROT_DIM:]], axis=-1)


def diff_attention(q, k, v, lam):
    b, s = q.shape[0], q.shape[1]
    nq = s // Q_BLOCK
    qb = q.reshape(b, nq, Q_BLOCK, N_HEADS, 2, HEAD_DIM).transpose(1, 0, 2, 3, 4, 5)
    scale = HEAD_DIM ** -0.5

    def one_block(qi):
        sc = jnp.einsum('bqhcd,bkhcd->bhcqk', qi, k).astype(jnp.float32) * scale
        p = jax.nn.softmax(sc, axis=-1)
        a = p[:, :, 0] - lam * p[:, :, 1]
        return jnp.einsum('bhqk,bkhe->bqhe', a.astype(v.dtype), v)

    o = lax.map(one_block, qb)
    return o.transpose(1, 0, 2, 3, 4).reshape(b, s, N_HEADS, V_DIM)


def conformer_conv(u, w_dw, b_dw, g_ln, b_ln):
    a, gate = jnp.split(u, 2, axis=-1)
    z = a * jax.nn.sigmoid(gate)
    z = lax.conv_general_dilated(z, w_dw[:, None, :].astype(z.dtype), window_strides=(1,),
                                 padding=[(CONV_PAD, CONV_PAD)],
                                 dimension_numbers=('NWC', 'WIO', 'NWC'),
                                 feature_group_count=CONV_WIDTH) + b_dw
    return jax.nn.silu(layer_norm(z, g_ln, b_ln))


def hier_moe(h, w_rg, b_rg, w_re, b_re, w_gate_up, w_down):
    t, d = h.shape
    lg = (h @ w_rg).astype(jnp.float32) + b_rg
    pg = jax.nn.softmax(lg, axis=-1)
    g_sel = jnp.argmax(lg, axis=-1)
    p_group = jnp.take_along_axis(pg, g_sel[:, None], axis=1)[:, 0]
    le = ((h @ w_re).astype(jnp.float32) + b_re).reshape(t, N_GROUPS, EXPERTS_PER_GROUP)
    le_g = jnp.take_along_axis(le, g_sel[:, None, None], axis=1)[:, 0]
    pe = jax.nn.softmax(le_g, axis=-1)
    top_p, top_i = lax.top_k(pe, TOP_K)
    top_p = top_p / jnp.sum(top_p, axis=-1, keepdims=True)
    gates = p_group[:, None] * top_p
    eids = g_sel[:, None] * EXPERTS_PER_GROUP + top_i

    n_assign = t * TOP_K
    e_flat = eids.reshape(n_assign).astype(jnp.int32)
    t_flat = jnp.repeat(jnp.arange(t, dtype=jnp.int32), TOP_K)
    w_flat = gates.reshape(n_assign)
    order = jnp.argsort(e_flat)
    e_s, t_s, w_s = e_flat[order], t_flat[order], w_flat[order]
    counts = jax.ops.segment_sum(jnp.ones((n_assign,), jnp.int32), e_flat, num_segments=N_EXPERTS)
    starts = jnp.cumsum(counts) - counts
    padded = ((counts + MOE_BLOCK - 1) // MOE_BLOCK) * MOE_BLOCK
    pad_ends = jnp.cumsum(padded)
    pad_starts = pad_ends - padded
    rank = jnp.arange(n_assign, dtype=jnp.int32) - starts[e_s]
    dest = pad_starts[e_s] + rank
    n_rows = n_assign + N_EXPERTS * MOE_BLOCK
    n_blk = n_rows // MOE_BLOCK
    row_tok = jnp.full((n_rows,), t, jnp.int32).at[dest].set(t_s)
    row_w = jnp.zeros((n_rows,), jnp.float32).at[dest].set(w_s)
    blk_start = jnp.arange(n_blk, dtype=jnp.int32) * MOE_BLOCK
    blk_eid = jnp.minimum(jnp.searchsorted(pad_ends, blk_start, side='right'), N_EXPERTS - 1).astype(jnp.int32)
    h_pad = jnp.concatenate([h, jnp.zeros((1, d), h.dtype)], axis=0)
    xg = h_pad[row_tok].reshape(n_blk, MOE_BLOCK, d)

    def expert_block(args):
        xb, e = args
        gu = xb @ w_gate_up[e]
        g_, u_ = jnp.split(gu, 2, axis=-1)
        return (jax.nn.silu(g_) * u_) @ w_down[e]

    yb = lax.map(expert_block, (xg, blk_eid)).reshape(n_rows, d)
    y = jax.ops.segment_sum(yb * row_w[:, None].astype(yb.dtype), row_tok, num_segments=t + 1)
    return y[:t]


def encoder_layer(x, c, p, lam_init):
    b, s, d = x.shape
    mod = jax.nn.silu(c) @ p['w_ada'] + p['b_ada']
    sh1, sc1, g1, sh2, sc2, g2 = jnp.split(mod, 6, axis=-1)

    h = modulate(rms_norm(x, p['g_norm1']), sh1, sc1)
    proj = h @ p['w_in']
    q, k, v, u = jnp.split(proj, [QK_WIDTH, 2 * QK_WIDTH, 2 * QK_WIDTH + ATTN_WIDTH], axis=-1)
    pos = jnp.arange(s, dtype=jnp.float32)
    q = apply_rotary(rms_norm(q.reshape(b, s, N_HEADS, 2, HEAD_DIM), p['g_q']), pos)
    k = apply_rotary(rms_norm(k.reshape(b, s, N_HEADS, 2, HEAD_DIM), p['g_k']), pos)
    v = v.reshape(b, s, N_HEADS, V_DIM)
    lam = (jnp.exp(jnp.sum(p['lambda_q1'].astype(jnp.float32) * p['lambda_k1'].astype(jnp.float32)))
           - jnp.exp(jnp.sum(p['lambda_q2'].astype(jnp.float32) * p['lambda_k2'].astype(jnp.float32)))
           + lam_init)
    o = diff_attention(q, k, v, lam)
    o = (rms_norm(o, p['g_subln']) * (1.0 - lam_init)).reshape(b, s, ATTN_WIDTH)
    cv = conformer_conv(u, p['w_dw'], p['b_dw'], p['g_conv_ln'], p['b_conv_ln'])
    mix = jnp.concatenate([o, cv], axis=-1) @ p['w_out']
    x = x + g1[:, None, :] * mix

    h2 = modulate(rms_norm(x, p['g_norm2']), sh2, sc2)
    y = hier_moe(h2.reshape(b * s, d), p['w_router_group'], p['b_router_group'],
                 p['w_router_expert'], p['b_router_expert'], p['w_gate_up'], p['w_down'])
    return x + g2[:, None, :] * y.reshape(b, s, d)


def setup_inputs(seed: int = 0) -> dict:
    key = jax.random.key(seed)
    ks = jax.random.split(key, 32)
    L, D = DEPTH, D_MODEL

    def nrm(k, shape, scale):
        return jax.random.normal(k, shape, jnp.float32) * scale

    return {
        'x_prompt': nrm(ks[0], (BATCH, SEQ, D), 1.0),
        'x_sample': nrm(ks[1], (DEC_BATCH, DEC_SEQ, D), 1.0),
        'c_prompt': nrm(ks[2], (BATCH, D), 1.0),
        'c_sample': nrm(ks[3], (DEC_BATCH, D), 1.0),
        'w_ada': nrm(ks[4], (L, D, 6 * D), 0.1 * D ** -0.5),
        'b_ada': nrm(ks[5], (L, 6 * D), 0.01),
        'g_norm1': 1.0 + nrm(ks[6], (L, D), 0.02),
        'w_in': nrm(ks[7], (L, D, IN_WIDTH), D ** -0.5),
        'g_q': 1.0 + nrm(ks[8], (L, HEAD_DIM), 0.02),
        'g_k': 1.0 + nrm(ks[9], (L, HEAD_DIM), 0.02),
        'lambda_q1': nrm(ks[10], (L, HEAD_DIM), 0.1),
        'lambda_k1': nrm(ks[11], (L, HEAD_DIM), 0.1),
        'lambda_q2': nrm(ks[12], (L, HEAD_DIM), 0.1),
        'lambda_k2': nrm(ks[13], (L, HEAD_DIM), 0.1),
        'g_subln': 1.0 + nrm(ks[14], (L, V_DIM), 0.02),
        'w_dw': nrm(ks[15], (L, CONV_KERNEL, CONV_WIDTH), CONV_KERNEL ** -0.5),
        'b_dw': nrm(ks[16], (L, CONV_WIDTH), 0.01),
        'g_conv_ln': 1.0 + nrm(ks[17], (L, CONV_WIDTH), 0.02),
        'b_conv_ln': nrm(ks[18], (L, CONV_WIDTH), 0.01),
        'w_out': nrm(ks[19], (L, D, D), D ** -0.5),
        'g_norm2': 1.0 + nrm(ks[20], (L, D), 0.02),
        'w_router_group': nrm(ks[21], (L, D, N_GROUPS), D ** -0.5),
        'b_router_group': nrm(ks[22], (L, N_GROUPS), 0.01),
        'w_router_expert': nrm(ks[23], (L, D, N_EXPERTS), D ** -0.5),
        'b_router_expert': nrm(ks[24], (L, N_EXPERTS), 0.01),
        'w_gate_up': nrm(ks[25], (L, N_EXPERTS, D, 2 * EXPERT_FF), D ** -0.5),
        'w_down': nrm(ks[26], (L, N_EXPERTS, EXPERT_FF, D), EXPERT_FF ** -0.5),
    }


def reference(x_prompt, x_sample, c_prompt, c_sample, w_ada, b_ada, g_norm1, w_in, g_q, g_k,
              lambda_q1, lambda_k1, lambda_q2, lambda_k2, g_subln, w_dw, b_dw, g_conv_ln, b_conv_ln,
              w_out, g_norm2, w_router_group, b_router_group, w_router_expert, b_router_expert,
              w_gate_up, w_down):
    xp, xs = x_prompt, x_sample
    for l in range(DEPTH):
        lam_init = 0.8 - 0.6 * math.exp(-0.3 * l)
        p = dict(w_ada=w_ada[l], b_ada=b_ada[l], g_norm1=g_norm1[l], w_in=w_in[l], g_q=g_q[l], g_k=g_k[l],
                 lambda_q1=lambda_q1[l], lambda_k1=lambda_k1[l], lambda_q2=lambda_q2[l], lambda_k2=lambda_k2[l],
                 g_subln=g_subln[l], w_dw=w_dw[l], b_dw=b_dw[l], g_conv_ln=g_conv_ln[l], b_conv_ln=b_conv_ln[l],
                 w_out=w_out[l], g_norm2=g_norm2[l], w_router_group=w_router_group[l],
                 b_router_group=b_router_group[l], w_router_expert=w_router_expert[l],
                 b_router_expert=b_router_expert[l], w_gate_up=w_gate_up[l], w_down=w_down[l])
        xp = encoder_layer(xp, c_prompt, p, lam_init)
        xs = encoder_layer(xs, c_sample, p, lam_init)
    return (xp, xs)
```

```python
import functools
import math

import numpy as np
import jax
import jax.numpy as jnp
from jax import lax
from jax.experimental import pallas as pl
from jax.experimental.pallas import tpu as pltpu

F32 = jnp.float32
BF16 = jnp.bfloat16

N_HEADS = 4
HEAD_DIM = 64
V_DIM = 128
ROT_DIM = 16
ROPE_THETA = 500000.0
CONV_KERNEL = 31
CONV_PAD = CONV_KERNEL // 2
N_GROUPS = 4
EXPERTS_PER_GROUP = 8
N_PAIRS = EXPERTS_PER_GROUP * (EXPERTS_PER_GROUP - 1) // 2
N_CLASSES = N_GROUPS * N_PAIRS
EPS = 1e-6
LOG2E = 1.4426950408889634
NEG_BIG = -1e30

VMEM_LIMIT_BYTES = 56 * 1024 * 1024
TOK_TILE = 512
INFO_W = 128
Q_TILE = 256
KEY_BLOCKS_PER_LOOP = 8
CONV_TILE = 512
CONV_CHUNK = 64
HALO = 16
ROUTE_TILE = 512
ROW_TILE = 512
MOE_BLOCK = 256


def _cparams(*sem):
    return pltpu.CompilerParams(dimension_semantics=sem, vmem_limit_bytes=VMEM_LIMIT_BYTES)


def _ada_kernel(c_ref, w_ref, b_ref, o_ref):
    c = c_ref[...]
    s = c * jax.nn.sigmoid(c)
    o_ref[...] = jnp.dot(s.astype(BF16), w_ref[...].astype(BF16), preferred_element_type=F32) + b_ref[...]


def _ada(c_all, w_ada, b_ada):
    nb, d = c_all.shape
    n = w_ada.shape[1]
    tn = n // 4
    return pl.pallas_call(
        _ada_kernel,
        grid=(n // tn,),
        in_specs=[pl.BlockSpec((nb, d), lambda j: (0, 0)),
                  pl.BlockSpec((d, tn), lambda j: (0, j)),
                  pl.BlockSpec((1, tn), lambda j: (0, j))],
        out_specs=pl.BlockSpec((nb, tn), lambda j: (0, j)),
        out_shape=jax.ShapeDtypeStruct((nb, n), F32),
        compiler_params=_cparams("arbitrary"),
        name="ada",
    )(c_all, w_ada, b_ada.reshape(1, n))


def _proj_kernel(x_ref, sh_ref, sc_ref, g1_ref, w_ref, seg_ref, gq_ref, gk_ref, cos_ref, sin_ref,
                 q_ref, k_ref, vt_ref, z_ref, *, q_scale):
    x = x_ref[...]
    ms = jnp.mean(x * x, axis=-1, keepdims=True)
    xn = x * lax.rsqrt(ms + EPS) * g1_ref[...]
    h = xn * (1.0 + sc_ref[...]) + sh_ref[...]
    proj = jnp.dot(h.astype(BF16), w_ref[...], preferred_element_type=F32)
    qw = N_HEADS * 2 * HEAD_DIM
    cos = cos_ref[...]
    sin = sin_ref[...]
    lane = lax.broadcasted_iota(jnp.int32, cos.shape, 1)
    first_half = jnp.bitwise_and(lane, HEAD_DIM - 1) < (ROT_DIM // 2)

    def norm_rot(t, g):
        ss = jnp.dot((t * t).astype(BF16), seg_ref[...], preferred_element_type=F32)
        tn = t * lax.rsqrt(ss * (1.0 / HEAD_DIM) + EPS) * g
        outs = []
        for j in range(qw // 128):
            tj = tn[:, j * 128:(j + 1) * 128]
            up = pltpu.roll(tj, 128 - ROT_DIM // 2, axis=1)
            dn = pltpu.roll(tj, ROT_DIM // 2, axis=1)
            outs.append(tj * cos + jnp.where(first_half, up, dn) * sin)
        return jnp.concatenate(outs, axis=1)

    q = norm_rot(proj[:, 0:qw], gq_ref[...])
    k = norm_rot(proj[:, qw:2 * qw], gk_ref[...])
    q_ref[...] = (q * q_scale).astype(BF16)
    k_ref[...] = k.astype(BF16)
    aw = N_HEADS * V_DIM
    v = proj[:, 2 * qw:2 * qw + aw]
    vt_ref[...] = v.T.astype(BF16)
    a = proj[:, 2 * qw + aw:2 * qw + 2 * aw]
    gate = proj[:, 2 * qw + 2 * aw:2 * qw + 3 * aw]
    z_ref[...] = a * jax.nn.sigmoid(gate)


def _proj(x, sh1, sc1, g1n, w_in, seg, gq, gk, cos_t, sin_t):
    b, s, d = x.shape
    tm = TOK_TILE
    qw = N_HEADS * 2 * HEAD_DIM
    aw = N_HEADS * V_DIM
    cw = w_in.shape[1] - 2 * qw - aw
    assert cw == 2 * aw and s % tm == 0
    row = lambda bb, i: (bb, 0, 0)
    const2 = lambda bb, i: (0, 0)
    tile = lambda bb, i: (bb, i, 0)
    kern = functools.partial(_proj_kernel, q_scale=HEAD_DIM ** -0.5 * LOG2E)
    return pl.pallas_call(
        kern,
        grid=(b, s // tm),
        in_specs=[pl.BlockSpec((None, tm, d), tile),
                  pl.BlockSpec((None, 1, d), row),
                  pl.BlockSpec((None, 1, d), row),
                  pl.BlockSpec((1, d), const2),
                  pl.BlockSpec(w_in.shape, const2),
                  pl.BlockSpec(seg.shape, const2),
                  pl.BlockSpec((1, qw), const2),
                  pl.BlockSpec((1, qw), const2),
                  pl.BlockSpec((tm, 128), lambda bb, i: (i, 0)),
                  pl.BlockSpec((tm, 128), lambda bb, i: (i, 0))],
        out_specs=[pl.BlockSpec((None, tm, qw), tile),
                   pl.BlockSpec((None, tm, qw), tile),
                   pl.BlockSpec((None, None, aw, tm), lambda bb, i: (bb, i, 0, 0)),
                   pl.BlockSpec((None, tm, aw), tile)],
        out_shape=[jax.ShapeDtypeStruct((b, s, qw), BF16),
                   jax.ShapeDtypeStruct((b, s, qw), BF16),
                   jax.ShapeDtypeStruct((b, s // tm, aw, tm), BF16),
                   jax.ShapeDtypeStruct((b, s, aw), F32)],
        compiler_params=_cparams("parallel", "parallel"),
        name="proj",
    )(x, sh1, sc1, g1n, w_in, seg, gq, gk, cos_t, sin_t)


def _attn_kernel(lam_ref, q_ref, k_ref, vt_ref, gs_ref, o_ref, qm_ref, s_ref, mb_ref, m_ref, l_ref, acc_ref,
                 *, sk, unroll, out_scale):
    q = q_ref[...]
    lane = lax.broadcasted_iota(jnp.int32, q.shape, 1)
    zero = jnp.zeros_like(q)
    qm_ref[0] = jnp.where(lane < HEAD_DIM, q, zero)
    qm_ref[1] = jnp.where(lane >= HEAD_DIM, q, zero)
    m_ref[...] = jnp.full(m_ref.shape, NEG_BIG, F32)
    l_ref[...] = jnp.zeros(l_ref.shape, F32)
    acc_ref[...] = jnp.zeros(acc_ref.shape, F32)
    nkb = vt_ref.shape[0]

    def scores(kb, slot):
        kblk = k_ref[pl.ds(pl.multiple_of(kb * sk, sk), sk), :]
        for c in range(2):
            s = lax.dot_general(kblk, qm_ref[c], (((1,), (1,)), ((), ())), preferred_element_type=F32)
            s_ref[slot, c] = s
            mb_ref[slot, c] = jnp.max(s, axis=0, keepdims=True)

    def accumulate(kb, slot):
        vblk = vt_ref[kb]
        for c in range(2):
            m_old = m_ref[c]
            m_new = jnp.maximum(m_old, mb_ref[slot, c])
            alpha = jnp.exp2(m_old - m_new)
            p = jnp.exp2(s_ref[slot, c] - m_new)
            l_ref[c] = alpha * l_ref[c] + jnp.sum(p, axis=0, keepdims=True)
            acc_ref[c] = alpha * acc_ref[c] + jnp.dot(vblk, p.astype(BF16), preferred_element_type=F32)
            m_ref[c] = m_new

    def group(kb0, last):
        for i in range(unroll):
            if not (last and i == unroll - 1):
                scores(kb0 + i + 1, (i + 1) % 2)
            accumulate(kb0 + i, i % 2)

    scores(0, 0)

    def body(j, carry):
        group(unroll * j, False)
        return carry

    lax.fori_loop(0, nkb // unroll - 1, body, 0)
    group(nkb - unroll, True)
    lam = lam_ref[0]
    o = acc_ref[0] * (1.0 / l_ref[0]) - lam * (acc_ref[1] * (1.0 / l_ref[1]))
    ms = jnp.mean(o * o, axis=0, keepdims=True)
    on = o * lax.rsqrt(ms + EPS) * (gs_ref[...] * out_scale)
    o_ref[...] = on.T.astype(BF16)


def _attention(lam, q, k, vt, gsub_col, out_scale):
    b, s, qw = q.shape
    nkb, aw, sk = vt.shape[1:]
    mq = Q_TILE
    unroll = min(nkb, KEY_BLOCKS_PER_LOOP)
    assert unroll % 2 == 0 and nkb % unroll == 0
    kern = functools.partial(_attn_kernel, sk=sk, unroll=unroll, out_scale=out_scale)
    return pl.pallas_call(
        kern,
        grid=(b, N_HEADS, s // mq),
        in_specs=[pl.BlockSpec(memory_space=pltpu.SMEM),
                  pl.BlockSpec((None, mq, 128), lambda bb, h, i: (bb, i, h)),
                  pl.BlockSpec((None, s, 128), lambda bb, h, i: (bb, 0, h)),
                  pl.BlockSpec((None, nkb, V_DIM, sk), lambda bb, h, i: (bb, 0, h, 0)),
                  pl.BlockSpec((V_DIM, 1), lambda bb, h, i: (0, 0))],
        out_specs=pl.BlockSpec((None, mq, V_DIM), lambda bb, h, i: (bb, i, h)),
        out_shape=jax.ShapeDtypeStruct((b, s, aw), BF16),
        scratch_shapes=[pltpu.VMEM((2, mq, 128), BF16), pltpu.VMEM((2, 2, sk, mq), F32),
                        pltpu.VMEM((2, 2, 1, mq), F32), pltpu.VMEM((2, 1, mq), F32), pltpu.VMEM((2, 1, mq), F32),
                        pltpu.VMEM((2, V_DIM, mq), F32)],
        compiler_params=_cparams("parallel", "parallel", "arbitrary"),
        name="attn",
    )(lam, q, k, vt, gsub_col)


def _conv_kernel(prev_ref, cur_ref, next_ref, w_ref, b_ref, g_ref, beta_ref, o_ref, ext_ref):
    i = pl.program_id(1)
    tc = cur_ref.shape[0]
    keep_prev = (i > 0).astype(F32)
    keep_next = (i < pl.num_programs(1) - 1).astype(F32)
    ext_ref[0:HALO, :] = prev_ref[...] * keep_prev
    ext_ref[HALO:HALO + tc, :] = cur_ref[...]
    ext_ref[HALO + tc:HALO + tc + HALO, :] = next_ref[...] * keep_next
    w = w_ref[...]
    base = HALO - CONV_PAD
    for r0 in range(0, tc, CONV_CHUNK):
        acc = jnp.broadcast_to(b_ref[...], (CONV_CHUNK, w.shape[1]))
        for j in range(CONV_KERNEL):
            acc = acc + w[j:j + 1, :] * ext_ref[r0 + base + j:r0 + base + j + CONV_CHUNK, :]
        mu = jnp.mean(acc, axis=-1, keepdims=True)
        cen = acc - mu
        var = jnp.mean(cen * cen, axis=-1, keepdims=True)
        y = cen * lax.rsqrt(var + EPS) * g_ref[...] + beta_ref[...]
        o_ref[r0:r0 + CONV_CHUNK, :] = (y * jax.nn.sigmoid(y)).astype(BF16)


def _conv(z, w_dw, b_dw, g_ln, b_ln):
    b, s, c = z.shape
    tc = CONV_TILE
    nh = tc // HALO
    last = s // HALO - 1
    vec = lambda bb, i: (0, 0)
    return pl.pallas_call(
        _conv_kernel,
        grid=(b, s // tc),
        in_specs=[pl.BlockSpec((None, HALO, c), lambda bb, i: (bb, jnp.maximum(i * nh - 1, 0), 0)),
                  pl.BlockSpec((None, tc, c), lambda bb, i: (bb, i, 0)),
                  pl.BlockSpec((None, HALO, c), lambda bb, i: (bb, jnp.minimum((i + 1) * nh, last), 0)),
                  pl.BlockSpec((CONV_KERNEL, c), vec),
                  pl.BlockSpec((1, c), vec), pl.BlockSpec((1, c), vec), pl.BlockSpec((1, c), vec)],
        out_specs=pl.BlockSpec((None, tc, c), lambda bb, i: (bb, i, 0)),
        out_shape=jax.ShapeDtypeStruct((b, s, c), BF16),
        scratch_shapes=[pltpu.VMEM((tc + 2 * HALO, c), F32)],
        compiler_params=_cparams("parallel", "parallel"),
        name="conv",
    )(z, z, z, w_dw, b_dw, g_ln, b_ln)


def _outproj_kernel(o_ref, cv_ref, x_ref, g1_ref, wo_ref, wc_ref, x1_ref):
    mix = jnp.dot(o_ref[...], wo_ref[...], preferred_element_type=F32)
    mix = mix + jnp.dot(cv_ref[...], wc_ref[...], preferred_element_type=F32)
    x1_ref[...] = x_ref[...] + g1_ref[...] * mix


def _outproj(o, cv, x, g1, w_o, w_c):
    b, s, d = x.shape
    tm = TOK_TILE
    aw = o.shape[-1]
    tile = lambda bb, i: (bb, i, 0)
    const2 = lambda bb, i: (0, 0)
    return pl.pallas_call(
        _outproj_kernel,
        grid=(b, s // tm),
        in_specs=[pl.BlockSpec((None, tm, aw), tile), pl.BlockSpec((None, tm, aw), tile),
                  pl.BlockSpec((None, tm, d), tile), pl.BlockSpec((None, 1, d), lambda bb, i: (bb, 0, 0)),
                  pl.BlockSpec(w_o.shape, const2), pl.BlockSpec(w_c.shape, const2)],
        out_specs=pl.BlockSpec((None, tm, d), tile),
        out_shape=jax.ShapeDtypeStruct((b, s, d), F32),
        compiler_params=_cparams("parallel", "parallel"),
        name="outproj",
    )(o, cv, x, g1, w_o, w_c)


def _route_kernel(x1_ref, sh_ref, sc_ref, g2n_ref, wr_ref, br_ref, tri_ref, h2x_ref, cnt_ref, carry_ref):
    first = jnp.logical_and(pl.program_id(0) == 0, pl.program_id(1) == 0)

    @pl.when(first)
    def _():
        carry_ref[...] = jnp.zeros(carry_ref.shape, F32)

    x = x1_ref[...]
    d = x.shape[1]
    ms = jnp.mean(x * x, axis=-1, keepdims=True)
    h2 = x * lax.rsqrt(ms + EPS) * g2n_ref[...] * (1.0 + sc_ref[...]) + sh_ref[...]
    h2x_ref[:, 0:d] = h2
    lg = jnp.dot(h2.astype(BF16), wr_ref[...], preferred_element_type=F32) + br_ref[...]
    lane = lax.broadcasted_iota(jnp.int32, lg.shape, 1).astype(F32)
    far = jnp.float32(1e9)
    in_g = lane < N_GROUPS
    lgm = jnp.where(in_g, lg, NEG_BIG)
    gmax = jnp.max(lgm, axis=1, keepdims=True)
    gsel = jnp.min(jnp.where(lgm == gmax, lane, far), axis=1, keepdims=True)
    p_group = 1.0 / jnp.sum(jnp.where(in_g, jnp.exp(lg - gmax), 0.0), axis=1, keepdims=True)
    lo_lane = N_GROUPS + gsel * EXPERTS_PER_GROUP
    in_e = jnp.logical_and(lane >= lo_lane, lane < lo_lane + EXPERTS_PER_GROUP)
    le = jnp.where(in_e, lg, NEG_BIG)
    v1 = jnp.max(le, axis=1, keepdims=True)
    l1 = jnp.min(jnp.where(le == v1, lane, far), axis=1, keepdims=True)
    le2 = jnp.where(lane == l1, NEG_BIG, le)
    v2 = jnp.max(le2, axis=1, keepdims=True)
    l2 = jnp.min(jnp.where(le2 == v2, lane, far), axis=1, keepdims=True)
    r = jnp.exp(v2 - v1)
    w1 = p_group / (1.0 + r)
    w2 = w1 * r
    i1 = l1 - lo_lane
    i2 = l2 - lo_lane
    a = jnp.minimum(i1, i2)
    b = jnp.maximum(i1, i2)
    w_lo = jnp.where(i1 < i2, w1, w2)
    w_hi = jnp.where(i1 < i2, w2, w1)
    cls = gsel * N_PAIRS + a * (2 * EXPERTS_PER_GROUP - 1 - a) * 0.5 + (b - a - 1.0)
    onehot = (lane == cls).astype(F32)
    excl = jnp.dot(tri_ref[...], onehot.astype(BF16), preferred_element_type=F32)
    rank = jnp.sum(onehot * (excl + carry_ref[...]), axis=1, keepdims=True)
    carry_ref[...] = carry_ref[...] + jnp.sum(onehot, axis=0, keepdims=True)
    info = jnp.where(lane == 0, cls, jnp.where(lane == 1, rank, jnp.where(lane == 2, w_lo,
                                                                           jnp.where(lane == 3, w_hi, 0.0))))
    h2x_ref[:, d:d + INFO_W] = info
    cnt_ref[...] = jnp.broadcast_to(carry_ref[...], cnt_ref.shape)


def _route(x1, sh2, sc2, g2n, wr, br, tri):
    b, s, d = x1.shape
    tr = ROUTE_TILE
    tile = lambda bb, i: (bb, i, 0)
    row = lambda bb, i: (bb, 0, 0)
    const2 = lambda bb, i: (0, 0)
    return pl.pallas_call(
        _route_kernel,
        grid=(b, s // tr),
        in_specs=[pl.BlockSpec((None, tr, d), tile), pl.BlockSpec((None, 1, d), row), pl.BlockSpec((None, 1, d), row),
                  pl.BlockSpec((1, d), const2), pl.BlockSpec(wr.shape, const2), pl.BlockSpec((1, 128), const2),
                  pl.BlockSpec((tr, tr), const2)],
        out_specs=[pl.BlockSpec((None, tr, d + INFO_W), tile), pl.BlockSpec((8, 128), const2)],
        out_shape=[jax.ShapeDtypeStruct((b, s, d + INFO_W), F32), jax.ShapeDtypeStruct((8, 128), F32)],
        scratch_shapes=[pltpu.VMEM((1, 128), F32)],
        compiler_params=_cparams("arbitrary", "arbitrary"),
        name="route",
    )(x1, sh2, sc2, g2n, wr, br, tri)


def _row_copy(src_ref, dst_ref, src_row, dst_row, sem):
    return pltpu.make_async_copy(src_ref.at[pl.ds(src_row, 1), :], dst_ref.at[pl.ds(dst_row, 1), :], sem)


def _dispatch_kernel(dest_ref, h_ref, xs_in, xs_ref, sem):
    del xs_in
    n = h_ref.shape[0]

    def start(r, c):
        _row_copy(h_ref, xs_ref, r, dest_ref[0, r], sem.at[0]).start()
        return c

    lax.fori_loop(0, n, start, 0, unroll=8)

    def wait(r, c):
        _row_copy(h_ref, xs_ref, 0, 0, sem.at[0]).wait()
        return c

    lax.fori_loop(0, n, wait, 0, unroll=8)


def _dispatch(dest3, h2x, xs):
    t, w = h2x.shape
    tr = ROW_TILE
    return pl.pallas_call(
        _dispatch_kernel,
        grid=(t // tr,),
        in_specs=[pl.BlockSpec((None, 1, tr), lambda i: (i, 0, 0), memory_space=pltpu.SMEM),
                  pl.BlockSpec((tr, w), lambda i: (i, 0)),
                  pl.BlockSpec(memory_space=pl.ANY)],
        out_specs=pl.BlockSpec(memory_space=pl.ANY),
        out_shape=jax.ShapeDtypeStruct(xs.shape, F32),
        scratch_shapes=[pltpu.SemaphoreType.DMA((1,))],
        input_output_aliases={2: 0},
        compiler_params=_cparams("arbitrary"),
        name="dispatch",
    )(dest3, h2x, xs)


def _moe_kernel(elo_ref, ehi_ref, nused_ref, xs_ref, wgu_lo, wd_lo, wgu_hi, wd_hi, ys_ref):
    del elo_ref, ehi_ref
    used = pl.program_id(0) < nused_ref[0]
    d = ys_ref.shape[1]

    @pl.when(used)
    def _():
        x = xs_ref[:, 0:d].astype(BF16)
        info = xs_ref[:, d:d + INFO_W]
        ff = wd_lo.shape[0]

        def expert(wgu, wd):
            gu = jnp.dot(x, wgu[...], preferred_element_type=F32)
            g = gu[:, :ff]
            u = gu[:, ff:]
            act = g * jax.nn.sigmoid(g) * u
            return jnp.dot(act.astype(BF16), wd[...], preferred_element_type=F32)

        ys_ref[...] = info[:, 2:3] * expert(wgu_lo, wd_lo) + info[:, 3:4] * expert(wgu_hi, wd_hi)

    @pl.when(jnp.logical_not(used))
    def _():
        ys_ref[...] = jnp.zeros(ys_ref.shape, F32)


def _moe(blk_elo, blk_ehi, n_used, xs, w_gu, w_d):
    n_rows, w = xs.shape
    d = w - INFO_W
    tb = MOE_BLOCK
    n_blk = n_rows // tb
    ff = w_d.shape[1]
    grid_spec = pltpu.PrefetchScalarGridSpec(
        num_scalar_prefetch=3,
        grid=(n_blk,),
        in_specs=[pl.BlockSpec((tb, w), lambda i, elo, ehi, nu: (jnp.minimum(i, nu[0] - 1), 0)),
                  pl.BlockSpec((None, d, 2 * ff), lambda i, elo, ehi, nu: (elo[i], 0, 0)),
                  pl.BlockSpec((None, ff, d), lambda i, elo, ehi, nu: (elo[i], 0, 0)),
                  pl.BlockSpec((None, d, 2 * ff), lambda i, elo, ehi, nu: (ehi[i], 0, 0)),
                  pl.BlockSpec((None, ff, d), lambda i, elo, ehi, nu: (ehi[i], 0, 0))],
        out_specs=pl.BlockSpec((tb, d), lambda i, elo, ehi, nu: (i, 0)),
    )
    return pl.pallas_call(
        _moe_kernel,
        grid_spec=grid_spec,
        out_shape=jax.ShapeDtypeStruct((n_rows, d), F32),
        compiler_params=_cparams("arbitrary"),
        name="moe",
    )(blk_elo, blk_ehi, n_used, xs, w_gu, w_d, w_gu, w_d)


def _combine_kernel(dest_ref, x1_ref, g2_ref, ys_ref, o_ref, buf_ref, sem):
    n = x1_ref.shape[0]

    def start(r, c):
        _row_copy(ys_ref, buf_ref, dest_ref[0, r], r, sem.at[0]).start()
        return c

    lax.fori_loop(0, n, start, 0, unroll=8)

    def wait(r, c):
        _row_copy(ys_ref, buf_ref, 0, 0, sem.at[0]).wait()
        return c

    lax.fori_loop(0, n, wait, 0, unroll=8)
    o_ref[...] = x1_ref[...] + g2_ref[...] * buf_ref[...]


def _combine(dest4, x1, g2, ys):
    b, s, d = x1.shape
    tr = ROW_TILE
    return pl.pallas_call(
        _combine_kernel,
        grid=(b, s // tr),
        in_specs=[pl.BlockSpec((None, None, 1, tr), lambda bb, i: (bb, i, 0, 0), memory_space=pltpu.SMEM),
                  pl.BlockSpec((None, tr, d), lambda bb, i: (bb, i, 0)),
                  pl.BlockSpec((None, 1, d), lambda bb, i: (bb, 0, 0)),
                  pl.BlockSpec(memory_space=pl.ANY)],
        out_specs=pl.BlockSpec((None, tr, d), lambda bb, i: (bb, i, 0)),
        out_shape=jax.ShapeDtypeStruct((b, s, d), F32),
        scratch_shapes=[pltpu.VMEM((tr, d), F32), pltpu.SemaphoreType.DMA((1,))],
        compiler_params=_cparams("arbitrary", "arbitrary"),
        name="combine",
    )(dest4, x1, g2, ys)


def _class_tables():
    elo, ehi = [], []
    for g in range(N_GROUPS):
        for a in range(EXPERTS_PER_GROUP):
            for b in range(a + 1, EXPERTS_PER_GROUP):
                elo.append(g * EXPERTS_PER_GROUP + a)
                ehi.append(g * EXPERTS_PER_GROUP + b)
    return np.asarray(elo, np.int32), np.asarray(ehi, np.int32)


def _rope_tables(s):
    inv_freq = ROPE_THETA ** (-jnp.arange(0, ROT_DIM, 2, dtype=F32) / ROT_DIM)
    ang = jnp.arange(s, dtype=F32)[:, None] * inv_freq[None, :]
    cos8, sin8 = jnp.cos(ang), jnp.sin(ang)
    ones = jnp.ones((s, HEAD_DIM - ROT_DIM), F32)
    cos64 = jnp.concatenate([cos8, cos8, ones], axis=1)
    sin64 = jnp.concatenate([-sin8, sin8, 0.0 * ones], axis=1)
    return jnp.tile(cos64, (1, 128 // HEAD_DIM)), jnp.tile(sin64, (1, 128 // HEAD_DIM))


def kernel(x_prompt, x_sample, c_prompt, c_sample, w_ada, b_ada, g_norm1, w_in, g_q, g_k, lambda_q1, lambda_k1, lambda_q2, lambda_k2, g_subln, w_dw, b_dw, g_conv_ln, b_conv_ln, w_out, g_norm2, w_router_group, b_router_group, w_router_expert, b_router_expert, w_gate_up, w_down):
    depth = w_ada.shape[0]
    d = x_prompt.shape[-1]
    qw = N_HEADS * 2 * HEAD_DIM
    aw = N_HEADS * V_DIM
    elo_tab, ehi_tab = _class_tables()
    seg = jnp.asarray(np.kron(np.eye(qw // HEAD_DIM), np.ones((HEAD_DIM, HEAD_DIM))), BF16)
    tri = jnp.asarray(np.tril(np.ones((ROUTE_TILE, ROUTE_TILE)), -1), BF16)
    groups = [x_prompt, x_sample]
    conds = [c_prompt, c_sample]
    nb = [c.shape[0] for c in conds]

    for l in range(depth):
        lam_init = 0.8 - 0.6 * math.exp(-0.3 * l)
        lam = (jnp.exp(jnp.sum(lambda_q1[l] * lambda_k1[l])) - jnp.exp(jnp.sum(lambda_q2[l] * lambda_k2[l]))
               + lam_init).reshape(1).astype(F32)
        mod = _ada(jnp.concatenate(conds, axis=0), w_ada[l], b_ada[l])
        w_in_b = w_in[l].astype(BF16)
        w_o_b = w_out[l][:aw].astype(BF16)
        w_c_b = w_out[l][aw:].astype(BF16)
        w_gu_b = w_gate_up[l].astype(BF16)
        w_d_b = w_down[l].astype(BF16)
        gq = jnp.tile(g_q[l], qw // HEAD_DIM).reshape(1, qw)
        gk = jnp.tile(g_k[l], qw // HEAD_DIM).reshape(1, qw)
        n_r = N_GROUPS + N_GROUPS * EXPERTS_PER_GROUP
        wr = jnp.zeros((d, 128), F32).at[:, :N_GROUPS].set(w_router_group[l]).at[:, N_GROUPS:n_r].set(
            w_router_expert[l]).astype(BF16)
        br = jnp.zeros((1, 128), F32).at[0, :N_GROUPS].set(b_router_group[l]).at[0, N_GROUPS:n_r].set(
            b_router_expert[l])

        x1s, h2xs, infos, counts, g2s = [], [], [], [], []
        row0 = 0
        for x, n in zip(groups, nb):
            b, s, _ = x.shape
            m = mod[row0:row0 + n].reshape(n, 1, 6 * d)
            row0 += n
            sh1, sc1, g1, sh2, sc2, g2 = [m[:, :, i * d:(i + 1) * d] for i in range(6)]
            cos_t, sin_t = _rope_tables(s)
            q, k, vt, z = _proj(x, sh1, sc1, g_norm1[l].reshape(1, d), w_in_b, seg, gq, gk, cos_t, sin_t)
            o = _attention(lam, q, k, vt, g_subln[l].reshape(V_DIM, 1), 1.0 - lam_init)
            cv = _conv(z, w_dw[l], b_dw[l].reshape(1, -1), g_conv_ln[l].reshape(1, -1), b_conv_ln[l].reshape(1, -1))
            x1 = _outproj(o, cv, x, g1, w_o_b, w_c_b)
            h2x, cnt = _route(x1, sh2, sc2, g_norm2[l].reshape(1, d), wr, br, tri)
            x1s.append(x1)
            h2xs.append(h2x.reshape(b * s, d + INFO_W))
            infos.append(h2xs[-1][:, d:d + 2])
            counts.append(cnt[0, :N_CLASSES].astype(jnp.int32))
            g2s.append(g2)

        t_all = sum(h.shape[0] for h in h2xs)
        n_rows = t_all + N_CLASSES * MOE_BLOCK
        n_blk = n_rows // MOE_BLOCK
        tot = sum(counts)
        padded = ((tot + MOE_BLOCK - 1) // MOE_BLOCK) * MOE_BLOCK
        pad_end = jnp.cumsum(padded)
        pad_start = pad_end - padded
        dests, offs = [], jnp.zeros_like(tot)
        for info, cnt in zip(infos, counts):
            cls = info[:, 0].astype(jnp.int32)
            rank = info[:, 1].astype(jnp.int32)
            dests.append(pad_start[cls] + offs[cls] + rank)
            offs = offs + cnt
        blk_cls = jnp.minimum(jnp.searchsorted(pad_end, jnp.arange(n_blk, dtype=jnp.int32) * MOE_BLOCK, side='right'),
                              N_CLASSES - 1)
        blk_elo = jnp.asarray(elo_tab)[blk_cls]
        blk_ehi = jnp.asarray(ehi_tab)[blk_cls]
        n_used = (pad_end[-1:] // MOE_BLOCK).astype(jnp.int32)

        xs = jnp.zeros((n_rows, d + INFO_W), F32)
        for h2x, dest in zip(h2xs, dests):
            xs = _dispatch(dest.astype(jnp.int32).reshape(h2x.shape[0] // ROW_TILE, 1, ROW_TILE), h2x, xs)
        ys = _moe(blk_elo, blk_ehi, n_used, xs, w_gu_b, w_d_b)
        outs = []
        for x1, dest, g2 in zip(x1s, dests, g2s):
            b, s, _ = x1.shape
            outs.append(_combine(dest.reshape(b, s // ROW_TILE, 1, ROW_TILE), x1, g2, ys))
        groups = outs
    return tuple(groups)
```

```python
import functools
import math

import numpy as np
import jax
import jax.numpy as jnp
from jax import lax
from jax.experimental import pallas as pl
from jax.experimental.pallas import tpu as pltpu

F32 = jnp.float32
BF16 = jnp.bfloat16

N_HEADS = 4
HEAD_DIM = 64
V_DIM = 128
ROT_DIM = 16
ROPE_THETA = 500000.0
CONV_KERNEL = 31
CONV_PAD = CONV_KERNEL // 2
N_GROUPS = 4
EXPERTS_PER_GROUP = 8
N_PAIRS = EXPERTS_PER_GROUP * (EXPERTS_PER_GROUP - 1) // 2
N_CLASSES = N_GROUPS * N_PAIRS
EPS = 1e-6
LOG2E = 1.4426950408889634
NEG_BIG = -1e30

VMEM_LIMIT_BYTES = 58 * 1024 * 1024
TOK_TILE = 512
INFO_W = 128
Q_TILE = 256
KEY_BLOCKS_PER_LOOP = 8
CONV_TILE = 512
CONV_CHUNK = 64
CONV_SHIFT_CHUNK = 136
HALO = 16
ROUTE_TILE = 512
ROW_TILE = 512
MOE_BLOCK = 256


def _cparams(*sem):
    return pltpu.CompilerParams(dimension_semantics=sem, vmem_limit_bytes=VMEM_LIMIT_BYTES)


def _ada_kernel(c_ref, w_ref, b_ref, o_ref):
    c = c_ref[...]
    s = c * jax.nn.sigmoid(c)
    o_ref[...] = jnp.dot(s.astype(BF16), w_ref[...].astype(BF16), preferred_element_type=F32) + b_ref[...]


def _ada(c_all, w_ada, b_ada):
    nb, d = c_all.shape
    n = w_ada.shape[1]
    tn = n // 4
    return pl.pallas_call(
        _ada_kernel,
        grid=(n // tn,),
        in_specs=[pl.BlockSpec((nb, d), lambda j: (0, 0)),
                  pl.BlockSpec((d, tn), lambda j: (0, j)),
                  pl.BlockSpec((1, tn), lambda j: (0, j))],
        out_specs=pl.BlockSpec((nb, tn), lambda j: (0, j)),
        out_shape=jax.ShapeDtypeStruct((nb, n), F32),
        compiler_params=_cparams("arbitrary"),
        name="ada",
    )(c_all, w_ada, b_ada.reshape(1, n))


def _proj_kernel(x_ref, sh_ref, sc_ref, g1_ref, w_ref, seg_ref, gq_ref, gk_ref, cos_ref, sin_ref,
                 q_ref, k_ref, vt_ref, z_ref, *, q_scale):
    x = x_ref[...]
    ms = jnp.mean(x * x, axis=-1, keepdims=True)
    xn = x * lax.rsqrt(ms + EPS) * g1_ref[...]
    h = xn * (1.0 + sc_ref[...]) + sh_ref[...]
    proj = jnp.dot(h.astype(BF16), w_ref[...], preferred_element_type=F32)
    qw = N_HEADS * 2 * HEAD_DIM
    cos = cos_ref[...]
    sin = sin_ref[...]
    lane = lax.broadcasted_iota(jnp.int32, cos.shape, 1)
    first_half = jnp.bitwise_and(lane, HEAD_DIM - 1) < (ROT_DIM // 2)

    def norm_rot(t, g):
        ss = jnp.dot((t * t).astype(BF16), seg_ref[...], preferred_element_type=F32)
        tn = t * lax.rsqrt(ss * (1.0 / HEAD_DIM) + EPS) * g
        outs = []
        for j in range(qw // 128):
            tj = tn[:, j * 128:(j + 1) * 128]
            up = pltpu.roll(tj, 128 - ROT_DIM // 2, axis=1)
            dn = pltpu.roll(tj, ROT_DIM // 2, axis=1)
            outs.append(tj * cos + jnp.where(first_half, up, dn) * sin)
        return jnp.concatenate(outs, axis=1)

    q = norm_rot(proj[:, 0:qw], gq_ref[...])
    k = norm_rot(proj[:, qw:2 * qw], gk_ref[...])
    q_ref[...] = (q * q_scale).astype(BF16)
    k_ref[...] = k.astype(BF16)
    aw = N_HEADS * V_DIM
    v = proj[:, 2 * qw:2 * qw + aw]
    vt_ref[...] = v.T.astype(BF16)
    a = proj[:, 2 * qw + aw:2 * qw + 2 * aw]
    gate = proj[:, 2 * qw + 2 * aw:2 * qw + 3 * aw]
    z_ref[...] = a * jax.nn.sigmoid(gate)


def _proj(x, sh1, sc1, g1n, w_in, seg, gq, gk, cos_t, sin_t):
    b, s, d = x.shape
    tm = TOK_TILE
    qw = N_HEADS * 2 * HEAD_DIM
    aw = N_HEADS * V_DIM
    cw = w_in.shape[1] - 2 * qw - aw
    assert cw == 2 * aw and s % tm == 0
    row = lambda bb, i: (bb, 0, 0)
    const2 = lambda bb, i: (0, 0)
    tile = lambda bb, i: (bb, i, 0)
    kern = functools.partial(_proj_kernel, q_scale=HEAD_DIM ** -0.5 * LOG2E)
    return pl.pallas_call(
        kern,
        grid=(b, s // tm),
        in_specs=[pl.BlockSpec((None, tm, d), tile),
                  pl.BlockSpec((None, 1, d), row),
                  pl.BlockSpec((None, 1, d), row),
                  pl.BlockSpec((1, d), const2),
                  pl.BlockSpec(w_in.shape, const2),
                  pl.BlockSpec(seg.shape, const2),
                  pl.BlockSpec((1, qw), const2),
                  pl.BlockSpec((1, qw), const2),
                  pl.BlockSpec((tm, 128), lambda bb, i: (i, 0)),
                  pl.BlockSpec((tm, 128), lambda bb, i: (i, 0))],
        out_specs=[pl.BlockSpec((None, tm, qw), tile),
                   pl.BlockSpec((None, tm, qw), tile),
                   pl.BlockSpec((None, None, aw, tm), lambda bb, i: (bb, i, 0, 0)),
                   pl.BlockSpec((None, tm, aw), tile)],
        out_shape=[jax.ShapeDtypeStruct((b, s, qw), BF16),
                   jax.ShapeDtypeStruct((b, s, qw), BF16),
                   jax.ShapeDtypeStruct((b, s // tm, aw, tm), BF16),
                   jax.ShapeDtypeStruct((b, s, aw), F32)],
        compiler_params=_cparams("parallel", "parallel"),
        name="proj",
    )(x, sh1, sc1, g1n, w_in, seg, gq, gk, cos_t, sin_t)


def _attn_kernel(lam_ref, q_ref, k_ref, vt_ref, gs_ref, o_ref, s_ref, mb_ref, m_ref, l_ref, acc_ref,
                 *, sk, mq, unroll, out_scale):
    nkb = vt_ref.shape[0]
    nq = q_ref.shape[0] // mq
    npairs = nq * nkb
    kb_bits = nkb.bit_length() - 1

    def init(qi, carry):
        m_ref[qi] = jnp.full(m_ref.shape[1:], NEG_BIG, F32)
        l_ref[qi] = jnp.zeros(l_ref.shape[1:], F32)
        acc_ref[qi] = jnp.zeros(acc_ref.shape[1:], F32)
        return carry

    lax.fori_loop(0, nq, init, 0)

    def scores(t, slot):
        qi = lax.shift_right_logical(t, kb_bits)
        kb = jnp.bitwise_and(t, nkb - 1)
        q = q_ref[pl.ds(pl.multiple_of(qi * mq, mq), mq), :]
        lane = lax.broadcasted_iota(jnp.int32, q.shape, 1)
        zero = jnp.zeros_like(q)
        kblk = k_ref[pl.ds(pl.multiple_of(kb * sk, sk), sk), :]
        for c in range(2):
            qm = jnp.where((lane < HEAD_DIM) if c == 0 else (lane >= HEAD_DIM), q, zero)
            s = lax.dot_general(kblk, qm, (((1,), (1,)), ((), ())), preferred_element_type=F32)
            s_ref[slot, c] = s
            mb_ref[slot, c] = jnp.max(s, axis=0, keepdims=True)

    def accumulate(qi, kb, slot):
        vblk = vt_ref[kb]
        for c in range(2):
            m_old = m_ref[qi, c]
            m_new = jnp.maximum(m_old, mb_ref[slot, c])
            alpha = jnp.exp2(m_old - m_new)
            p = jnp.exp2(s_ref[slot, c] - m_new)
            l_ref[qi, c] = alpha * l_ref[qi, c] + jnp.sum(p, axis=0, keepdims=True)
            acc_ref[qi, c] = alpha * acc_ref[qi, c] + jnp.dot(vblk, p.astype(BF16), preferred_element_type=F32)
            m_ref[qi, c] = m_new

    def group(t0, last):
        qi = lax.shift_right_logical(t0, kb_bits)
        kb0 = jnp.bitwise_and(t0, nkb - 1)
        for i in range(unroll):
            if not (last and i == unroll - 1):
                scores(t0 + i + 1, (i + 1) % 2)
            accumulate(qi, kb0 + i, i % 2)

    scores(jnp.int32(0), 0)

    def body(j, carry):
        group(unroll * j, False)
        return carry

    lax.fori_loop(0, npairs // unroll - 1, body, 0)
    group(jnp.int32(npairs - unroll), True)
    lam = lam_ref[0]

    def finish(qi, carry):
        o = acc_ref[qi, 0] * (1.0 / l_ref[qi, 0]) - lam * (acc_ref[qi, 1] * (1.0 / l_ref[qi, 1]))
        ms = jnp.mean(o * o, axis=0, keepdims=True)
        on = o * lax.rsqrt(ms + EPS) * (gs_ref[...] * out_scale)
        o_ref[pl.ds(pl.multiple_of(qi * mq, mq), mq), :] = on.T.astype(BF16)
        return carry

    lax.fori_loop(0, nq, finish, 0)


def _attention(lam, q, k, vt, gsub_col, out_scale):
    b, s, qw = q.shape
    nkb, aw, sk = vt.shape[1:]
    mq = Q_TILE
    nq = s // mq
    unroll = min(nkb, KEY_BLOCKS_PER_LOOP)
    assert unroll % 2 == 0 and nkb % unroll == 0 and nkb & (nkb - 1) == 0
    kern = functools.partial(_attn_kernel, sk=sk, mq=mq, unroll=unroll, out_scale=out_scale)
    head = lambda bb, h: (bb, 0, h)
    return pl.pallas_call(
        kern,
        grid=(b, N_HEADS),
        in_specs=[pl.BlockSpec(memory_space=pltpu.SMEM),
                  pl.BlockSpec((None, s, 128), head),
                  pl.BlockSpec((None, s, 128), head),
                  pl.BlockSpec((None, nkb, V_DIM, sk), lambda bb, h: (bb, 0, h, 0)),
                  pl.BlockSpec((V_DIM, 1), lambda bb, h: (0, 0))],
        out_specs=pl.BlockSpec((None, s, V_DIM), head),
        out_shape=jax.ShapeDtypeStruct((b, s, aw), BF16),
        scratch_shapes=[pltpu.VMEM((2, 2, sk, mq), F32), pltpu.VMEM((2, 2, 1, mq), F32),
                        pltpu.VMEM((nq, 2, 1, mq), F32), pltpu.VMEM((nq, 2, 1, mq), F32),
                        pltpu.VMEM((nq, 2, V_DIM, mq), F32)],
        compiler_params=_cparams("parallel", "parallel"),
        name="attn",
    )(lam, q, k, vt, gsub_col)


def _conv_kernel(prev_ref, cur_ref, next_ref, w_ref, b_ref, g_ref, beta_ref, o_ref, ext_ref, sh_ref):
    i = pl.program_id(1)
    tc, c = cur_ref.shape
    n_ext = tc + 2 * HALO
    keep_prev = (i > 0).astype(F32)
    keep_next = (i < pl.num_programs(1) - 1).astype(F32)
    ext_ref[0:HALO, :] = prev_ref[...] * keep_prev
    ext_ref[HALO:HALO + tc, :] = cur_ref[...]
    ext_ref[HALO + tc:n_ext, :] = next_ref[...] * keep_next
    ext_ref[n_ext:n_ext + 8, :] = jnp.zeros((8, c), F32)
    for o in range(1, 8):
        for e0 in range(0, n_ext, CONV_SHIFT_CHUNK):
            sh_ref[o - 1, e0:e0 + CONV_SHIFT_CHUNK, :] = ext_ref[e0 + o:e0 + o + CONV_SHIFT_CHUNK, :]
    w = w_ref[...]
    base = HALO - CONV_PAD
    for r0 in range(0, tc, CONV_CHUNK):
        acc = jnp.broadcast_to(b_ref[...], (CONV_CHUNK, c))
        for j in range(CONV_KERNEL):
            o = (base + j) % 8
            a0 = r0 + base + j - o
            src = ext_ref if o == 0 else sh_ref.at[o - 1]
            acc = acc + w[j:j + 1, :] * src[a0:a0 + CONV_CHUNK, :]
        mu = jnp.mean(acc, axis=-1, keepdims=True)
        cen = acc - mu
        var = jnp.mean(cen * cen, axis=-1, keepdims=True)
        y = cen * lax.rsqrt(var + EPS) * g_ref[...] + beta_ref[...]
        o_ref[r0:r0 + CONV_CHUNK, :] = (y * jax.nn.sigmoid(y)).astype(BF16)


def _conv(z, w_dw, b_dw, g_ln, b_ln):
    b, s, c = z.shape
    tc = CONV_TILE
    nh = tc // HALO
    last = s // HALO - 1
    vec = lambda bb, i: (0, 0)
    return pl.pallas_call(
        _conv_kernel,
        grid=(b, s // tc),
        in_specs=[pl.BlockSpec((None, HALO, c), lambda bb, i: (bb, jnp.maximum(i * nh - 1, 0), 0)),
                  pl.BlockSpec((None, tc, c), lambda bb, i: (bb, i, 0)),
                  pl.BlockSpec((None, HALO, c), lambda bb, i: (bb, jnp.minimum((i + 1) * nh, last), 0)),
                  pl.BlockSpec((CONV_KERNEL, c), vec),
                  pl.BlockSpec((1, c), vec), pl.BlockSpec((1, c), vec), pl.BlockSpec((1, c), vec)],
        out_specs=pl.BlockSpec((None, tc, c), lambda bb, i: (bb, i, 0)),
        out_shape=jax.ShapeDtypeStruct((b, s, c), BF16),
        scratch_shapes=[pltpu.VMEM((tc + 2 * HALO + 8, c), F32), pltpu.VMEM((7, tc + 2 * HALO, c), F32)],
        compiler_params=_cparams("parallel", "parallel"),
        name="conv",
    )(z, z, z, w_dw, b_dw, g_ln, b_ln)


def _outproj_kernel(o_ref, cv_ref, x_ref, g1_ref, wo_ref, wc_ref, x1_ref):
    mix = jnp.dot(o_ref[...], wo_ref[...], preferred_element_type=F32)
    mix = mix + jnp.dot(cv_ref[...], wc_ref[...], preferred_element_type=F32)
    x1_ref[...] = x_ref[...] + g1_ref[...] * mix


def _outproj(o, cv, x, g1, w_o, w_c):
    b, s, d = x.shape
    tm = TOK_TILE
    aw = o.shape[-1]
    tile = lambda bb, i: (bb, i, 0)
    const2 = lambda bb, i: (0, 0)
    return pl.pallas_call(
        _outproj_kernel,
        grid=(b, s // tm),
        in_specs=[pl.BlockSpec((None, tm, aw), tile), pl.BlockSpec((None, tm, aw), tile),
                  pl.BlockSpec((None, tm, d), tile), pl.BlockSpec((None, 1, d), lambda bb, i: (bb, 0, 0)),
                  pl.BlockSpec(w_o.shape, const2), pl.BlockSpec(w_c.shape, const2)],
        out_specs=pl.BlockSpec((None, tm, d), tile),
        out_shape=jax.ShapeDtypeStruct((b, s, d), F32),
        compiler_params=_cparams("parallel", "parallel"),
        name="outproj",
    )(o, cv, x, g1, w_o, w_c)


def _route_kernel(x1_ref, sh_ref, sc_ref, g2n_ref, wr_ref, br_ref, tri_ref, h2x_ref, cnt_ref, carry_ref):
    first = jnp.logical_and(pl.program_id(0) == 0, pl.program_id(1) == 0)

    @pl.when(first)
    def _():
        carry_ref[...] = jnp.zeros(carry_ref.shape, F32)

    x = x1_ref[...]
    d = x.shape[1]
    ms = jnp.mean(x * x, axis=-1, keepdims=True)
    h2 = x * lax.rsqrt(ms + EPS) * g2n_ref[...] * (1.0 + sc_ref[...]) + sh_ref[...]
    h2x_ref[:, 0:d] = h2
    lg = jnp.dot(h2.astype(BF16), wr_ref[...], preferred_element_type=F32) + br_ref[...]
    lane = lax.broadcasted_iota(jnp.int32, lg.shape, 1).astype(F32)
    far = jnp.float32(1e9)
    in_g = lane < N_GROUPS
    lgm = jnp.where(in_g, lg, NEG_BIG)
    gmax = jnp.max(lgm, axis=1, keepdims=True)
    gsel = jnp.min(jnp.where(lgm == gmax, lane, far), axis=1, keepdims=True)
    p_group = 1.0 / jnp.sum(jnp.where(in_g, jnp.exp(lg - gmax), 0.0), axis=1, keepdims=True)
    lo_lane = N_GROUPS + gsel * EXPERTS_PER_GROUP
    in_e = jnp.logical_and(lane >= lo_lane, lane < lo_lane + EXPERTS_PER_GROUP)
    le = jnp.where(in_e, lg, NEG_BIG)
    v1 = jnp.max(le, axis=1, keepdims=True)
    l1 = jnp.min(jnp.where(le == v1, lane, far), axis=1, keepdims=True)
    le2 = jnp.where(lane == l1, NEG_BIG, le)
    v2 = jnp.max(le2, axis=1, keepdims=True)
    l2 = jnp.min(jnp.where(le2 == v2, lane, far), axis=1, keepdims=True)
    r = jnp.exp(v2 - v1)
    w1 = p_group / (1.0 + r)
    w2 = w1 * r
    i1 = l1 - lo_lane
    i2 = l2 - lo_lane
    a = jnp.minimum(i1, i2)
    b = jnp.maximum(i1, i2)
    w_lo = jnp.where(i1 < i2, w1, w2)
    w_hi = jnp.where(i1 < i2, w2, w1)
    cls = gsel * N_PAIRS + a * (2 * EXPERTS_PER_GROUP - 1 - a) * 0.5 + (b - a - 1.0)
    onehot = (lane == cls).astype(F32)
    excl = jnp.dot(tri_ref[...], onehot.astype(BF16), preferred_element_type=F32)
    rank = jnp.sum(onehot * (excl + carry_ref[...]), axis=1, keepdims=True)
    carry_ref[...] = carry_ref[...] + jnp.sum(onehot, axis=0, keepdims=True)
    info = jnp.where(lane == 0, cls, jnp.where(lane == 1, rank, jnp.where(lane == 2, w_lo,
                                                                           jnp.where(lane == 3, w_hi, 0.0))))
    h2x_ref[:, d:d + INFO_W] = info
    cnt_ref[...] = jnp.broadcast_to(carry_ref[...], cnt_ref.shape)


def _route(x1, sh2, sc2, g2n, wr, br, tri):
    b, s, d = x1.shape
    tr = ROUTE_TILE
    tile = lambda bb, i: (bb, i, 0)
    row = lambda bb, i: (bb, 0, 0)
    const2 = lambda bb, i: (0, 0)
    return pl.pallas_call(
        _route_kernel,
        grid=(b, s // tr),
        in_specs=[pl.BlockSpec((None, tr, d), tile), pl.BlockSpec((None, 1, d), row), pl.BlockSpec((None, 1, d), row),
                  pl.BlockSpec((1, d), const2), pl.BlockSpec(wr.shape, const2), pl.BlockSpec((1, 128), const2),
                  pl.BlockSpec((tr, tr), const2)],
        out_specs=[pl.BlockSpec((None, tr, d + INFO_W), tile), pl.BlockSpec((8, 128), const2)],
        out_shape=[jax.ShapeDtypeStruct((b, s, d + INFO_W), F32), jax.ShapeDtypeStruct((8, 128), F32)],
        scratch_shapes=[pltpu.VMEM((1, 128), F32)],
        compiler_params=_cparams("arbitrary", "arbitrary"),
        name="route",
    )(x1, sh2, sc2, g2n, wr, br, tri)


def _dest_kernel(info_ref, start_ref, dest_ref):
    info_t = info_ref[...].T
    cls = info_t[0:1, :]
    rank = info_t[1:2, :]
    cid = lax.broadcasted_iota(jnp.int32, info_t.shape, 0).astype(F32)
    base = jnp.sum(jnp.where(cid == cls, start_ref[...], 0.0), axis=0, keepdims=True)
    dest_ref[...] = (base + rank).astype(jnp.int32)


def _dest(h2x, start_col):
    t, w = h2x.shape
    tr = ROW_TILE
    return pl.pallas_call(
        _dest_kernel,
        grid=(t // tr,),
        in_specs=[pl.BlockSpec((tr, INFO_W), lambda i: (i, (w - INFO_W) // INFO_W)),
                  pl.BlockSpec((INFO_W, 1), lambda i: (0, 0))],
        out_specs=pl.BlockSpec((None, 1, tr), lambda i: (i, 0, 0)),
        out_shape=jax.ShapeDtypeStruct((t // tr, 1, tr), jnp.int32),
        compiler_params=_cparams("parallel"),
        name="dest",
    )(h2x, start_col)


def _row_copy(src_ref, dst_ref, src_row, dst_row, sem):
    return pltpu.make_async_copy(src_ref.at[pl.ds(src_row, 1), :], dst_ref.at[pl.ds(dst_row, 1), :], sem)


def _dispatch_kernel(dest_ref, h_ref, xs_in, xs_ref, sem):
    del xs_in
    n = h_ref.shape[0]

    def start(r, c):
        _row_copy(h_ref, xs_ref, r, dest_ref[0, r], sem.at[0]).start()
        return c

    lax.fori_loop(0, n, start, 0, unroll=8)

    def wait(r, c):
        _row_copy(h_ref, xs_ref, 0, 0, sem.at[0]).wait()
        return c

    lax.fori_loop(0, n, wait, 0, unroll=8)


def _dispatch(dest3, h2x, xs):
    t, w = h2x.shape
    tr = ROW_TILE
    return pl.pallas_call(
        _dispatch_kernel,
        grid=(t // tr,),
        in_specs=[pl.BlockSpec((None, 1, tr), lambda i: (i, 0, 0), memory_space=pltpu.SMEM),
                  pl.BlockSpec((tr, w), lambda i: (i, 0)),
                  pl.BlockSpec(memory_space=pl.ANY)],
        out_specs=pl.BlockSpec(memory_space=pl.ANY),
        out_shape=jax.ShapeDtypeStruct(xs.shape, F32),
        scratch_shapes=[pltpu.SemaphoreType.DMA((1,))],
        input_output_aliases={2: 0},
        compiler_params=_cparams("arbitrary"),
        name="dispatch",
    )(dest3, h2x, xs)


def _moe_kernel(elo_ref, ehi_ref, nused_ref, xs_ref, wgu_lo, wd_lo, wgu_hi, wd_hi, ys_ref):
    del elo_ref, ehi_ref
    used = pl.program_id(0) < nused_ref[0]
    d = ys_ref.shape[1]

    @pl.when(used)
    def _():
        x = xs_ref[:, 0:d].astype(BF16)
        info = xs_ref[:, d:d + INFO_W]
        ff = wd_lo.shape[0]

        def expert(wgu, wd):
            gu = jnp.dot(x, wgu[...], preferred_element_type=F32)
            g = gu[:, :ff]
            u = gu[:, ff:]
            act = g * jax.nn.sigmoid(g) * u
            return jnp.dot(act.astype(BF16), wd[...], preferred_element_type=F32)

        ys_ref[...] = info[:, 2:3] * expert(wgu_lo, wd_lo) + info[:, 3:4] * expert(wgu_hi, wd_hi)

    @pl.when(jnp.logical_not(used))
    def _():
        ys_ref[...] = jnp.zeros(ys_ref.shape, F32)


def _moe(blk_elo, blk_ehi, n_used, xs, w_gu, w_d):
    n_rows, w = xs.shape
    d = w - INFO_W
    tb = MOE_BLOCK
    n_blk = n_rows // tb
    ff = w_d.shape[1]
    grid_spec = pltpu.PrefetchScalarGridSpec(
        num_scalar_prefetch=3,
        grid=(n_blk,),
        in_specs=[pl.BlockSpec((tb, w), lambda i, elo, ehi, nu: (jnp.minimum(i, nu[0] - 1), 0)),
                  pl.BlockSpec((None, d, 2 * ff), lambda i, elo, ehi, nu: (elo[i], 0, 0)),
                  pl.BlockSpec((None, ff, d), lambda i, elo, ehi, nu: (elo[i], 0, 0)),
                  pl.BlockSpec((None, d, 2 * ff), lambda i, elo, ehi, nu: (ehi[i], 0, 0)),
                  pl.BlockSpec((None, ff, d), lambda i, elo, ehi, nu: (ehi[i], 0, 0))],
        out_specs=pl.BlockSpec((tb, d), lambda i, elo, ehi, nu: (i, 0)),
    )
    return pl.pallas_call(
        _moe_kernel,
        grid_spec=grid_spec,
        out_shape=jax.ShapeDtypeStruct((n_rows, d), F32),
        compiler_params=_cparams("arbitrary"),
        name="moe",
    )(blk_elo, blk_ehi, n_used, xs, w_gu, w_d, w_gu, w_d)


def _combine_kernel(dest_ref, x1_ref, g2_ref, ys_ref, o_ref, buf_ref, sem):
    n = x1_ref.shape[0]

    def start(r, c):
        _row_copy(ys_ref, buf_ref, dest_ref[0, r], r, sem.at[0]).start()
        return c

    lax.fori_loop(0, n, start, 0, unroll=8)

    def wait(r, c):
        _row_copy(ys_ref, buf_ref, 0, 0, sem.at[0]).wait()
        return c

    lax.fori_loop(0, n, wait, 0, unroll=8)
    o_ref[...] = x1_ref[...] + g2_ref[...] * buf_ref[...]


def _combine(dest4, x1, g2, ys):
    b, s, d = x1.shape
    tr = ROW_TILE
    return pl.pallas_call(
        _combine_kernel,
        grid=(b, s // tr),
        in_specs=[pl.BlockSpec((None, None, 1, tr), lambda bb, i: (bb, i, 0, 0), memory_space=pltpu.SMEM),
                  pl.BlockSpec((None, tr, d), lambda bb, i: (bb, i, 0)),
                  pl.BlockSpec((None, 1, d), lambda bb, i: (bb, 0, 0)),
                  pl.BlockSpec(memory_space=pl.ANY)],
        out_specs=pl.BlockSpec((None, tr, d), lambda bb, i: (bb, i, 0)),
        out_shape=jax.ShapeDtypeStruct((b, s, d), F32),
        scratch_shapes=[pltpu.VMEM((tr, d), F32), pltpu.SemaphoreType.DMA((1,))],
        compiler_params=_cparams("arbitrary", "arbitrary"),
        name="combine",
    )(dest4, x1, g2, ys)


def _class_tables():
    elo, ehi = [], []
    for g in range(N_GROUPS):
        for a in range(EXPERTS_PER_GROUP):
            for b in range(a + 1, EXPERTS_PER_GROUP):
                elo.append(g * EXPERTS_PER_GROUP + a)
                ehi.append(g * EXPERTS_PER_GROUP + b)
    return np.asarray(elo, np.int32), np.asarray(ehi, np.int32)


def _rope_tables(s):
    inv_freq = ROPE_THETA ** (-jnp.arange(0, ROT_DIM, 2, dtype=F32) / ROT_DIM)
    ang = jnp.arange(s, dtype=F32)[:, None] * inv_freq[None, :]
    cos8, sin8 = jnp.cos(ang), jnp.sin(ang)
    ones = jnp.ones((s, HEAD_DIM - ROT_DIM), F32)
    cos64 = jnp.concatenate([cos8, cos8, ones], axis=1)
    sin64 = jnp.concatenate([-sin8, sin8, 0.0 * ones], axis=1)
    return jnp.tile(cos64, (1, 128 // HEAD_DIM)), jnp.tile(sin64, (1, 128 // HEAD_DIM))


def kernel(x_prompt, x_sample, c_prompt, c_sample, w_ada, b_ada, g_norm1, w_in, g_q, g_k, lambda_q1, lambda_k1, lambda_q2, lambda_k2, g_subln, w_dw, b_dw, g_conv_ln, b_conv_ln, w_out, g_norm2, w_router_group, b_router_group, w_router_expert, b_router_expert, w_gate_up, w_down):
    depth = w_ada.shape[0]
    d = x_prompt.shape[-1]
    qw = N_HEADS * 2 * HEAD_DIM
    aw = N_HEADS * V_DIM
    elo_tab, ehi_tab = _class_tables()
    seg = jnp.asarray(np.kron(np.eye(qw // HEAD_DIM), np.ones((HEAD_DIM, HEAD_DIM))), BF16)
    tri = jnp.asarray(np.tril(np.ones((ROUTE_TILE, ROUTE_TILE)), -1), BF16)
    groups = [x_prompt, x_sample]
    conds = [c_prompt, c_sample]
    nb = [c.shape[0] for c in conds]

    for l in range(depth):
        lam_init = 0.8 - 0.6 * math.exp(-0.3 * l)
        lam = (jnp.exp(jnp.sum(lambda_q1[l] * lambda_k1[l])) - jnp.exp(jnp.sum(lambda_q2[l] * lambda_k2[l]))
               + lam_init).reshape(1).astype(F32)
        mod = _ada(jnp.concatenate(conds, axis=0), w_ada[l], b_ada[l])
        w_in_b = w_in[l].astype(BF16)
        w_o_b = w_out[l][:aw].astype(BF16)
        w_c_b = w_out[l][aw:].astype(BF16)
        w_gu_b = w_gate_up[l].astype(BF16)
        w_d_b = w_down[l].astype(BF16)
        gq = jnp.tile(g_q[l], qw // HEAD_DIM).reshape(1, qw)
        gk = jnp.tile(g_k[l], qw // HEAD_DIM).reshape(1, qw)
        n_r = N_GROUPS + N_GROUPS * EXPERTS_PER_GROUP
        wr = jnp.zeros((d, 128), F32).at[:, :N_GROUPS].set(w_router_group[l]).at[:, N_GROUPS:n_r].set(
            w_router_expert[l]).astype(BF16)
        br = jnp.zeros((1, 128), F32).at[0, :N_GROUPS].set(b_router_group[l]).at[0, N_GROUPS:n_r].set(
            b_router_expert[l])

        x1s, h2xs, counts, g2s = [], [], [], []
        row0 = 0
        for x, n in zip(groups, nb):
            b, s, _ = x.shape
            m = mod[row0:row0 + n].reshape(n, 1, 6 * d)
            row0 += n
            sh1, sc1, g1, sh2, sc2, g2 = [m[:, :, i * d:(i + 1) * d] for i in range(6)]
            cos_t, sin_t = _rope_tables(s)
            q, k, vt, z = _proj(x, sh1, sc1, g_norm1[l].reshape(1, d), w_in_b, seg, gq, gk, cos_t, sin_t)
            o = _attention(lam, q, k, vt, g_subln[l].reshape(V_DIM, 1), 1.0 - lam_init)
            cv = _conv(z, w_dw[l], b_dw[l].reshape(1, -1), g_conv_ln[l].reshape(1, -1), b_conv_ln[l].reshape(1, -1))
            x1 = _outproj(o, cv, x, g1, w_o_b, w_c_b)
            h2x, cnt = _route(x1, sh2, sc2, g_norm2[l].reshape(1, d), wr, br, tri)
            x1s.append(x1)
            h2xs.append(h2x.reshape(b * s, d + INFO_W))
            counts.append(cnt[0, :N_CLASSES].astype(jnp.int32))
            g2s.append(g2)

        t_all = sum(h.shape[0] for h in h2xs)
        n_rows = t_all + N_CLASSES * MOE_BLOCK
        n_blk = n_rows // MOE_BLOCK
        tot = sum(counts)
        padded = ((tot + MOE_BLOCK - 1) // MOE_BLOCK) * MOE_BLOCK
        pad_end = jnp.cumsum(padded)
        pad_start = pad_end - padded
        dests, offs = [], jnp.zeros_like(tot)
        for h2x, cnt in zip(h2xs, counts):
            start_col = jnp.zeros((INFO_W, 1), F32).at[:N_CLASSES, 0].set((pad_start + offs).astype(F32))
            dests.append(_dest(h2x, start_col))
            offs = offs + cnt
        blk_cls = jnp.minimum(jnp.searchsorted(pad_end, jnp.arange(n_blk, dtype=jnp.int32) * MOE_BLOCK, side='right'),
                              N_CLASSES - 1)
        blk_elo = jnp.asarray(elo_tab)[blk_cls]
        blk_ehi = jnp.asarray(ehi_tab)[blk_cls]
        n_used = (pad_end[-1:] // MOE_BLOCK).astype(jnp.int32)

        xs = jnp.zeros((n_rows, d + INFO_W), F32)
        for h2x, dest in zip(h2xs, dests):
            xs = _dispatch(dest, h2x, xs)
        ys = _moe(blk_elo, blk_ehi, n_used, xs, w_gu_b, w_d_b)
        outs = []
        for x1, dest, g2 in zip(x1s, dests, g2s):
            b, s, _ = x1.shape
            outs.append(_combine(dest.reshape(b, s // ROW_TILE, 1, ROW_TILE), x1, g2, ys))
        groups = outs
    return tuple(groups)
```

```python
import functools
import math

import numpy as np
import jax
import jax.numpy as jnp
from jax import lax
from jax.experimental import pallas as pl
from jax.experimental.pallas import tpu as pltpu

F32 = jnp.float32
BF16 = jnp.bfloat16

N_HEADS = 4
HEAD_DIM = 64
V_DIM = 128
VT_ROWS = V_DIM + 16
ROT_DIM = 16
ROPE_THETA = 500000.0
CONV_KERNEL = 31
CONV_PAD = CONV_KERNEL // 2
N_GROUPS = 4
EXPERTS_PER_GROUP = 8
N_PAIRS = EXPERTS_PER_GROUP * (EXPERTS_PER_GROUP - 1) // 2
N_CLASSES = N_GROUPS * N_PAIRS
EPS = 1e-6
LOG2E = 1.4426950408889634
NEG_BIG = -1e30

VMEM_LIMIT_BYTES = 58 * 1024 * 1024
TOK_TILE = 512
INFO_W = 128
Q_TILE = 256
KEY_BLOCKS_PER_LOOP = 8
CONV_TILE = 512
CONV_CHUNK = 64
CONV_SHIFT_CHUNK = 136
HALO = 16
ROUTE_TILE = 512
ROW_TILE = 512
MOE_BLOCK = 256


def _cparams(*sem):
    return pltpu.CompilerParams(dimension_semantics=sem, vmem_limit_bytes=VMEM_LIMIT_BYTES)


def _ada_kernel(c_ref, w_ref, b_ref, o_ref):
    c = c_ref[...]
    s = c * jax.nn.sigmoid(c)
    o_ref[...] = jnp.dot(s.astype(BF16), w_ref[...].astype(BF16), preferred_element_type=F32) + b_ref[...]


def _ada(c_all, w_ada, b_ada):
    nb, d = c_all.shape
    n = w_ada.shape[1]
    tn = n // 4
    return pl.pallas_call(
        _ada_kernel,
        grid=(n // tn,),
        in_specs=[pl.BlockSpec((nb, d), lambda j: (0, 0)),
                  pl.BlockSpec((d, tn), lambda j: (0, j)),
                  pl.BlockSpec((1, tn), lambda j: (0, j))],
        out_specs=pl.BlockSpec((nb, tn), lambda j: (0, j)),
        out_shape=jax.ShapeDtypeStruct((nb, n), F32),
        compiler_params=_cparams("arbitrary"),
        name="ada",
    )(c_all, w_ada, b_ada.reshape(1, n))


def _proj_kernel(x_ref, sh_ref, sc_ref, g1_ref, w_ref, seg_ref, gq_ref, gk_ref, cos_ref, sin_ref,
                 q_ref, k_ref, vt_ref, z_ref, *, q_scale):
    x = x_ref[...]
    ms = jnp.mean(x * x, axis=-1, keepdims=True)
    xn = x * lax.rsqrt(ms + EPS) * g1_ref[...]
    h = xn * (1.0 + sc_ref[...]) + sh_ref[...]
    proj = jnp.dot(h.astype(BF16), w_ref[...], preferred_element_type=F32)
    qw = N_HEADS * 2 * HEAD_DIM
    cos = cos_ref[...]
    sin = sin_ref[...]
    lane = lax.broadcasted_iota(jnp.int32, cos.shape, 1)
    first_half = jnp.bitwise_and(lane, HEAD_DIM - 1) < (ROT_DIM // 2)

    def norm_rot(t, g):
        ss = jnp.dot((t * t).astype(BF16), seg_ref[...], preferred_element_type=F32)
        tn = t * lax.rsqrt(ss * (1.0 / HEAD_DIM) + EPS) * g
        outs = []
        for j in range(qw // 128):
            tj = tn[:, j * 128:(j + 1) * 128]
            up = pltpu.roll(tj, 128 - ROT_DIM // 2, axis=1)
            dn = pltpu.roll(tj, ROT_DIM // 2, axis=1)
            outs.append(tj * cos + jnp.where(first_half, up, dn) * sin)
        return jnp.concatenate(outs, axis=1)

    q = norm_rot(proj[:, 0:qw], gq_ref[...])
    k = norm_rot(proj[:, qw:2 * qw], gk_ref[...])
    q_ref[...] = (q * q_scale).astype(BF16)
    k_ref[...] = k.astype(BF16)
    aw = N_HEADS * V_DIM
    v = proj[:, 2 * qw:2 * qw + aw]
    vt = v.T
    extra = (lax.broadcasted_iota(jnp.int32, (VT_ROWS - V_DIM, vt.shape[1]), 0) == 0).astype(F32)
    pieces = []
    for hh in range(N_HEADS):
        pieces += [vt[hh * V_DIM:(hh + 1) * V_DIM], extra]
    vt_ref[...] = jnp.concatenate(pieces, axis=0).astype(BF16)
    a = proj[:, 2 * qw + aw:2 * qw + 2 * aw]
    gate = proj[:, 2 * qw + 2 * aw:2 * qw + 3 * aw]
    z_ref[...] = a * jax.nn.sigmoid(gate)


def _proj(x, sh1, sc1, g1n, w_in, seg, gq, gk, cos_t, sin_t):
    b, s, d = x.shape
    tm = TOK_TILE
    qw = N_HEADS * 2 * HEAD_DIM
    aw = N_HEADS * V_DIM
    cw = w_in.shape[1] - 2 * qw - aw
    assert cw == 2 * aw and s % tm == 0
    row = lambda bb, i: (bb, 0, 0)
    const2 = lambda bb, i: (0, 0)
    tile = lambda bb, i: (bb, i, 0)
    kern = functools.partial(_proj_kernel, q_scale=HEAD_DIM ** -0.5 * LOG2E)
    return pl.pallas_call(
        kern,
        grid=(b, s // tm),
        in_specs=[pl.BlockSpec((None, tm, d), tile),
                  pl.BlockSpec((None, 1, d), row),
                  pl.BlockSpec((None, 1, d), row),
                  pl.BlockSpec((1, d), const2),
                  pl.BlockSpec(w_in.shape, const2),
                  pl.BlockSpec(seg.shape, const2),
                  pl.BlockSpec((1, qw), const2),
                  pl.BlockSpec((1, qw), const2),
                  pl.BlockSpec((tm, 128), lambda bb, i: (i, 0)),
                  pl.BlockSpec((tm, 128), lambda bb, i: (i, 0))],
        out_specs=[pl.BlockSpec((None, tm, qw), tile),
                   pl.BlockSpec((None, tm, qw), tile),
                   pl.BlockSpec((None, None, N_HEADS * VT_ROWS, tm), lambda bb, i: (bb, i, 0, 0)),
                   pl.BlockSpec((None, tm, aw), tile)],
        out_shape=[jax.ShapeDtypeStruct((b, s, qw), BF16),
                   jax.ShapeDtypeStruct((b, s, qw), BF16),
                   jax.ShapeDtypeStruct((b, s // tm, N_HEADS * VT_ROWS, tm), BF16),
                   jax.ShapeDtypeStruct((b, s, aw), F32)],
        compiler_params=_cparams("parallel", "parallel"),
        name="proj",
    )(x, sh1, sc1, g1n, w_in, seg, gq, gk, cos_t, sin_t)


def _attn_kernel(lam_ref, q_ref, k_ref, vt_ref, gs_ref, o_ref, s_ref, mb_ref, m_ref, acc_ref,
                 *, sk, mq, unroll, out_scale):
    nkb = vt_ref.shape[0]
    nq = q_ref.shape[0] // mq
    npairs = nq * nkb
    kb_bits = nkb.bit_length() - 1

    def init(qi, carry):
        m_ref[qi] = jnp.full(m_ref.shape[1:], NEG_BIG, F32)
        acc_ref[qi] = jnp.zeros(acc_ref.shape[1:], F32)
        return carry

    lax.fori_loop(0, nq, init, 0)

    def scores(t, slot):
        qi = lax.shift_right_logical(t, kb_bits)
        kb = jnp.bitwise_and(t, nkb - 1)
        q = q_ref[pl.ds(pl.multiple_of(qi * mq, mq), mq), :]
        lane = lax.broadcasted_iota(jnp.int32, q.shape, 1)
        zero = jnp.zeros_like(q)
        kblk = k_ref[pl.ds(pl.multiple_of(kb * sk, sk), sk), :]
        for c in range(2):
            qm = jnp.where((lane < HEAD_DIM) if c == 0 else (lane >= HEAD_DIM), q, zero)
            s = lax.dot_general(kblk, qm, (((1,), (1,)), ((), ())), preferred_element_type=F32)
            s_ref[slot, c] = s
            mb_ref[slot, c] = jnp.max(s, axis=0, keepdims=True)

    def accumulate(qi, kb, slot):
        vblk = vt_ref[kb]
        for c in range(2):
            m_old = m_ref[qi, c]
            m_new = jnp.maximum(m_old, mb_ref[slot, c])
            alpha = jnp.exp2(m_old - m_new)
            p = jnp.exp2(s_ref[slot, c] - m_new)
            acc_ref[qi, c] = alpha * acc_ref[qi, c] + jnp.dot(vblk, p.astype(BF16), preferred_element_type=F32)
            m_ref[qi, c] = m_new

    def group(t0, last):
        qi = lax.shift_right_logical(t0, kb_bits)
        kb0 = jnp.bitwise_and(t0, nkb - 1)
        for i in range(unroll):
            if not (last and i == unroll - 1):
                scores(t0 + i + 1, (i + 1) % 2)
            accumulate(qi, kb0 + i, i % 2)

    scores(jnp.int32(0), 0)

    def body(j, carry):
        group(unroll * j, False)
        return carry

    lax.fori_loop(0, npairs // unroll - 1, body, 0)
    group(jnp.int32(npairs - unroll), True)
    lam = lam_ref[0]

    def finish(qi, carry):
        a0 = acc_ref[qi, 0]
        a1 = acc_ref[qi, 1]
        o = (a0[:V_DIM] * (1.0 / a0[V_DIM:V_DIM + 1])
             - lam * (a1[:V_DIM] * (1.0 / a1[V_DIM:V_DIM + 1])))
        ms = jnp.mean(o * o, axis=0, keepdims=True)
        on = o * lax.rsqrt(ms + EPS) * (gs_ref[...] * out_scale)
        o_ref[pl.ds(pl.multiple_of(qi * mq, mq), mq), :] = on.T.astype(BF16)
        return carry

    lax.fori_loop(0, nq, finish, 0)


def _attention(lam, q, k, vt, gsub_col, out_scale):
    b, s, qw = q.shape
    nkb, _, sk = vt.shape[1:]
    aw = N_HEADS * V_DIM
    mq = Q_TILE
    nq = s // mq
    unroll = min(nkb, KEY_BLOCKS_PER_LOOP)
    assert unroll % 2 == 0 and nkb % unroll == 0 and nkb & (nkb - 1) == 0
    kern = functools.partial(_attn_kernel, sk=sk, mq=mq, unroll=unroll, out_scale=out_scale)
    head = lambda bb, h: (bb, 0, h)
    return pl.pallas_call(
        kern,
        grid=(b, N_HEADS),
        in_specs=[pl.BlockSpec(memory_space=pltpu.SMEM),
                  pl.BlockSpec((None, s, 128), head),
                  pl.BlockSpec((None, s, 128), head),
                  pl.BlockSpec((None, nkb, VT_ROWS, sk), lambda bb, h: (bb, 0, h, 0)),
                  pl.BlockSpec((V_DIM, 1), lambda bb, h: (0, 0))],
        out_specs=pl.BlockSpec((None, s, V_DIM), head),
        out_shape=jax.ShapeDtypeStruct((b, s, aw), BF16),
        scratch_shapes=[pltpu.VMEM((2, 2, sk, mq), F32), pltpu.VMEM((2, 2, 1, mq), F32),
                        pltpu.VMEM((nq, 2, 1, mq), F32), pltpu.VMEM((nq, 2, VT_ROWS, mq), F32)],
        compiler_params=_cparams("parallel", "parallel"),
        name="attn",
    )(lam, q, k, vt, gsub_col)


def _conv_kernel(prev_ref, cur_ref, next_ref, w_ref, b_ref, g_ref, beta_ref, o_ref, ext_ref, sh_ref):
    i = pl.program_id(1)
    tc, c = cur_ref.shape
    n_ext = tc + 2 * HALO
    keep_prev = (i > 0).astype(F32)
    keep_next = (i < pl.num_programs(1) - 1).astype(F32)
    ext_ref[0:HALO, :] = prev_ref[...] * keep_prev
    ext_ref[HALO:HALO + tc, :] = cur_ref[...]
    ext_ref[HALO + tc:n_ext, :] = next_ref[...] * keep_next
    ext_ref[n_ext:n_ext + 8, :] = jnp.zeros((8, c), F32)
    for o in range(1, 8):
        for e0 in range(0, n_ext, CONV_SHIFT_CHUNK):
            sh_ref[o - 1, e0:e0 + CONV_SHIFT_CHUNK, :] = ext_ref[e0 + o:e0 + o + CONV_SHIFT_CHUNK, :]
    w = w_ref[...]
    base = HALO - CONV_PAD
    for r0 in range(0, tc, CONV_CHUNK):
        acc = jnp.broadcast_to(b_ref[...], (CONV_CHUNK, c))
        for j in range(CONV_KERNEL):
            o = (base + j) % 8
            a0 = r0 + base + j - o
            src = ext_ref if o == 0 else sh_ref.at[o - 1]
            acc = acc + w[j:j + 1, :] * src[a0:a0 + CONV_CHUNK, :]
        mu = jnp.mean(acc, axis=-1, keepdims=True)
        cen = acc - mu
        var = jnp.mean(cen * cen, axis=-1, keepdims=True)
        y = cen * lax.rsqrt(var + EPS) * g_ref[...] + beta_ref[...]
        o_ref[r0:r0 + CONV_CHUNK, :] = (y * jax.nn.sigmoid(y)).astype(BF16)


def _conv(z, w_dw, b_dw, g_ln, b_ln):
    b, s, c = z.shape
    tc = CONV_TILE
    nh = tc // HALO
    last = s // HALO - 1
    vec = lambda bb, i: (0, 0)
    return pl.pallas_call(
        _conv_kernel,
        grid=(b, s // tc),
        in_specs=[pl.BlockSpec((None, HALO, c), lambda bb, i: (bb, jnp.maximum(i * nh - 1, 0), 0)),
                  pl.BlockSpec((None, tc, c), lambda bb, i: (bb, i, 0)),
                  pl.BlockSpec((None, HALO, c), lambda bb, i: (bb, jnp.minimum((i + 1) * nh, last), 0)),
                  pl.BlockSpec((CONV_KERNEL, c), vec),
                  pl.BlockSpec((1, c), vec), pl.BlockSpec((1, c), vec), pl.BlockSpec((1, c), vec)],
        out_specs=pl.BlockSpec((None, tc, c), lambda bb, i: (bb, i, 0)),
        out_shape=jax.ShapeDtypeStruct((b, s, c), BF16),
        scratch_shapes=[pltpu.VMEM((tc + 2 * HALO + 8, c), F32), pltpu.VMEM((7, tc + 2 * HALO, c), F32)],
        compiler_params=_cparams("parallel", "parallel"),
        name="conv",
    )(z, z, z, w_dw, b_dw, g_ln, b_ln)


def _outproj_route_kernel(o_ref, cv_ref, x_ref, g1_ref, wo_ref, wc_ref, sh_ref, sc_ref, g2n_ref, wr_ref, br_ref,
                          tri_ref, x1_ref, h2x_ref, cnt_ref, carry_ref):
    first = jnp.logical_and(pl.program_id(0) == 0, pl.program_id(1) == 0)

    @pl.when(first)
    def _():
        carry_ref[...] = jnp.zeros(carry_ref.shape, F32)

    mix = jnp.dot(o_ref[...], wo_ref[...], preferred_element_type=F32)
    mix = mix + jnp.dot(cv_ref[...], wc_ref[...], preferred_element_type=F32)
    x = x_ref[...] + g1_ref[...] * mix
    x1_ref[...] = x
    d = x.shape[1]
    ms = jnp.mean(x * x, axis=-1, keepdims=True)
    h2 = x * lax.rsqrt(ms + EPS) * g2n_ref[...] * (1.0 + sc_ref[...]) + sh_ref[...]
    h2x_ref[:, 0:d] = h2
    lg = jnp.dot(h2.astype(BF16), wr_ref[...], preferred_element_type=F32) + br_ref[...]
    lane = lax.broadcasted_iota(jnp.int32, lg.shape, 1).astype(F32)
    far = jnp.float32(1e9)
    in_g = lane < N_GROUPS
    lgm = jnp.where(in_g, lg, NEG_BIG)
    gmax = jnp.max(lgm, axis=1, keepdims=True)
    gsel = jnp.min(jnp.where(lgm == gmax, lane, far), axis=1, keepdims=True)
    p_group = 1.0 / jnp.sum(jnp.where(in_g, jnp.exp(lg - gmax), 0.0), axis=1, keepdims=True)
    lo_lane = N_GROUPS + gsel * EXPERTS_PER_GROUP
    in_e = jnp.logical_and(lane >= lo_lane, lane < lo_lane + EXPERTS_PER_GROUP)
    le = jnp.where(in_e, lg, NEG_BIG)
    v1 = jnp.max(le, axis=1, keepdims=True)
    l1 = jnp.min(jnp.where(le == v1, lane, far), axis=1, keepdims=True)
    le2 = jnp.where(lane == l1, NEG_BIG, le)
    v2 = jnp.max(le2, axis=1, keepdims=True)
    l2 = jnp.min(jnp.where(le2 == v2, lane, far), axis=1, keepdims=True)
    r = jnp.exp(v2 - v1)
    w1 = p_group / (1.0 + r)
    w2 = w1 * r
    i1 = l1 - lo_lane
    i2 = l2 - lo_lane
    a = jnp.minimum(i1, i2)
    b = jnp.maximum(i1, i2)
    w_lo = jnp.where(i1 < i2, w1, w2)
    w_hi = jnp.where(i1 < i2, w2, w1)
    cls = gsel * N_PAIRS + a * (2 * EXPERTS_PER_GROUP - 1 - a) * 0.5 + (b - a - 1.0)
    onehot = (lane == cls).astype(F32)
    excl = jnp.dot(tri_ref[...], onehot.astype(BF16), preferred_element_type=F32)
    rank = jnp.sum(onehot * (excl + carry_ref[...]), axis=1, keepdims=True)
    carry_ref[...] = carry_ref[...] + jnp.sum(onehot, axis=0, keepdims=True)
    info = jnp.where(lane == 0, cls, jnp.where(lane == 1, rank, jnp.where(lane == 2, w_lo,
                                                                           jnp.where(lane == 3, w_hi, 0.0))))
    h2x_ref[:, d:d + INFO_W] = info
    cnt_ref[...] = jnp.broadcast_to(carry_ref[...], cnt_ref.shape)


def _outproj_route(o, cv, x, g1, w_o, w_c, sh2, sc2, g2n, wr, br, tri):
    b, s, d = x.shape
    tr = ROUTE_TILE
    aw = o.shape[-1]
    tile = lambda bb, i: (bb, i, 0)
    row = lambda bb, i: (bb, 0, 0)
    const2 = lambda bb, i: (0, 0)
    return pl.pallas_call(
        _outproj_route_kernel,
        grid=(b, s // tr),
        in_specs=[pl.BlockSpec((None, tr, aw), tile), pl.BlockSpec((None, tr, aw), tile),
                  pl.BlockSpec((None, tr, d), tile), pl.BlockSpec((None, 1, d), row),
                  pl.BlockSpec(w_o.shape, const2), pl.BlockSpec(w_c.shape, const2),
                  pl.BlockSpec((None, 1, d), row), pl.BlockSpec((None, 1, d), row),
                  pl.BlockSpec((1, d), const2), pl.BlockSpec(wr.shape, const2), pl.BlockSpec((1, 128), const2),
                  pl.BlockSpec((tr, tr), const2)],
        out_specs=[pl.BlockSpec((None, tr, d), tile), pl.BlockSpec((None, tr, d + INFO_W), tile),
                   pl.BlockSpec((8, 128), const2)],
        out_shape=[jax.ShapeDtypeStruct((b, s, d), F32), jax.ShapeDtypeStruct((b, s, d + INFO_W), F32),
                   jax.ShapeDtypeStruct((8, 128), F32)],
        scratch_shapes=[pltpu.VMEM((1, 128), F32)],
        compiler_params=_cparams("arbitrary", "arbitrary"),
        name="outproj_route",
    )(o, cv, x, g1, w_o, w_c, sh2, sc2, g2n, wr, br, tri)


def _dest_kernel(info_ref, start_ref, dest_ref):
    info_t = info_ref[...].T
    cls = info_t[0:1, :]
    rank = info_t[1:2, :]
    cid = lax.broadcasted_iota(jnp.int32, info_t.shape, 0).astype(F32)
    base = jnp.sum(jnp.where(cid == cls, start_ref[...], 0.0), axis=0, keepdims=True)
    dest_ref[...] = (base + rank).astype(jnp.int32)


def _dest(h2x, start_col):
    t, w = h2x.shape
    tr = ROW_TILE
    return pl.pallas_call(
        _dest_kernel,
        grid=(t // tr,),
        in_specs=[pl.BlockSpec((tr, INFO_W), lambda i: (i, (w - INFO_W) // INFO_W)),
                  pl.BlockSpec((INFO_W, 1), lambda i: (0, 0))],
        out_specs=pl.BlockSpec((None, 1, tr), lambda i: (i, 0, 0)),
        out_shape=jax.ShapeDtypeStruct((t // tr, 1, tr), jnp.int32),
        compiler_params=_cparams("parallel"),
        name="dest",
    )(h2x, start_col)


def _row_copy(src_ref, dst_ref, src_row, dst_row, sem):
    return pltpu.make_async_copy(src_ref.at[pl.ds(src_row, 1), :], dst_ref.at[pl.ds(dst_row, 1), :], sem)


def _scatter_rows(dest_ref, h_ref, xs_ref, sem):
    n = h_ref.shape[0]

    def start(r, c):
        _row_copy(h_ref, xs_ref, r, dest_ref[0, r], sem).start()
        return c

    lax.fori_loop(0, n, start, 0, unroll=8)

    def wait(r, c):
        _row_copy(h_ref, xs_ref, 0, 0, sem).wait()
        return c

    lax.fori_loop(0, n, wait, 0, unroll=8)


def _dispatch_kernel(dest_ref, h_ref, xs_in, xs_ref, sem):
    del xs_in
    _scatter_rows(dest_ref, h_ref, xs_ref, sem.at[0])


def _dispatch(dest3, h2x, xs):
    t, w = h2x.shape
    tr = ROW_TILE
    return pl.pallas_call(
        _dispatch_kernel,
        grid=(t // tr,),
        in_specs=[pl.BlockSpec((None, 1, tr), lambda i: (i, 0, 0), memory_space=pltpu.SMEM),
                  pl.BlockSpec((tr, w), lambda i: (i, 0)),
                  pl.BlockSpec(memory_space=pl.ANY)],
        out_specs=pl.BlockSpec(memory_space=pl.ANY),
        out_shape=jax.ShapeDtypeStruct(xs.shape, F32),
        scratch_shapes=[pltpu.SemaphoreType.DMA((1,))],
        input_output_aliases={2: 0},
        compiler_params=_cparams("arbitrary"),
        name="dispatch",
    )(dest3, h2x, xs)


def _moe_kernel(elo_ref, ehi_ref, nused_ref, xs_ref, wgu_lo, wd_lo, wgu_hi, wd_hi, ys_ref):
    del elo_ref, ehi_ref
    used = pl.program_id(0) < nused_ref[0]
    d = ys_ref.shape[1]

    @pl.when(used)
    def _():
        x = xs_ref[:, 0:d].astype(BF16)
        info = xs_ref[:, d:d + INFO_W]
        ff = wd_lo.shape[0]

        def expert(wgu, wd):
            gu = jnp.dot(x, wgu[...], preferred_element_type=F32)
            g = gu[:, :ff]
            u = gu[:, ff:]
            act = g * jax.nn.sigmoid(g) * u
            return jnp.dot(act.astype(BF16), wd[...], preferred_element_type=F32)

        ys_ref[...] = info[:, 2:3] * expert(wgu_lo, wd_lo) + info[:, 3:4] * expert(wgu_hi, wd_hi)

    @pl.when(jnp.logical_not(used))
    def _():
        ys_ref[...] = jnp.zeros(ys_ref.shape, F32)


def _moe(blk_elo, blk_ehi, n_used, xs, w_gu, w_d):
    n_rows, w = xs.shape
    d = w - INFO_W
    tb = MOE_BLOCK
    n_blk = n_rows // tb
    ff = w_d.shape[1]
    grid_spec = pltpu.PrefetchScalarGridSpec(
        num_scalar_prefetch=3,
        grid=(n_blk,),
        in_specs=[pl.BlockSpec((tb, w), lambda i, elo, ehi, nu: (jnp.minimum(i, nu[0] - 1), 0)),
                  pl.BlockSpec((None, d, 2 * ff), lambda i, elo, ehi, nu: (elo[i], 0, 0)),
                  pl.BlockSpec((None, ff, d), lambda i, elo, ehi, nu: (elo[i], 0, 0)),
                  pl.BlockSpec((None, d, 2 * ff), lambda i, elo, ehi, nu: (ehi[i], 0, 0)),
                  pl.BlockSpec((None, ff, d), lambda i, elo, ehi, nu: (ehi[i], 0, 0))],
        out_specs=pl.BlockSpec((tb, d), lambda i, elo, ehi, nu: (i, 0)),
    )
    return pl.pallas_call(
        _moe_kernel,
        grid_spec=grid_spec,
        out_shape=jax.ShapeDtypeStruct((n_rows, d), F32),
        compiler_params=_cparams("arbitrary"),
        name="moe",
    )(blk_elo, blk_ehi, n_used, xs, w_gu, w_d, w_gu, w_d)


def _combine_kernel(dest_ref, x1_ref, g2_ref, ys_ref, o_ref, buf_ref, sem):
    n = x1_ref.shape[0]

    def start(r, c):
        _row_copy(ys_ref, buf_ref, dest_ref[0, r], r, sem.at[0]).start()
        return c

    lax.fori_loop(0, n, start, 0, unroll=8)

    def wait(r, c):
        _row_copy(ys_ref, buf_ref, 0, 0, sem.at[0]).wait()
        return c

    lax.fori_loop(0, n, wait, 0, unroll=8)
    o_ref[...] = x1_ref[...] + g2_ref[...] * buf_ref[...]


def _combine(dest4, x1, g2, ys):
    b, s, d = x1.shape
    tr = ROW_TILE
    return pl.pallas_call(
        _combine_kernel,
        grid=(b, s // tr),
        in_specs=[pl.BlockSpec((None, None, 1, tr), lambda bb, i: (bb, i, 0, 0), memory_space=pltpu.SMEM),
                  pl.BlockSpec((None, tr, d), lambda bb, i: (bb, i, 0)),
                  pl.BlockSpec((None, 1, d), lambda bb, i: (bb, 0, 0)),
                  pl.BlockSpec(memory_space=pl.ANY)],
        out_specs=pl.BlockSpec((None, tr, d), lambda bb, i: (bb, i, 0)),
        out_shape=jax.ShapeDtypeStruct((b, s, d), F32),
        scratch_shapes=[pltpu.VMEM((tr, d), F32), pltpu.SemaphoreType.DMA((1,))],
        compiler_params=_cparams("arbitrary", "arbitrary"),
        name="combine",
    )(dest4, x1, g2, ys)


def _class_tables():
    elo, ehi = [], []
    for g in range(N_GROUPS):
        for a in range(EXPERTS_PER_GROUP):
            for b in range(a + 1, EXPERTS_PER_GROUP):
                elo.append(g * EXPERTS_PER_GROUP + a)
                ehi.append(g * EXPERTS_PER_GROUP + b)
    return np.asarray(elo, np.int32), np.asarray(ehi, np.int32)


def _rope_tables(s):
    inv_freq = ROPE_THETA ** (-jnp.arange(0, ROT_DIM, 2, dtype=F32) / ROT_DIM)
    ang = jnp.arange(s, dtype=F32)[:, None] * inv_freq[None, :]
    cos8, sin8 = jnp.cos(ang), jnp.sin(ang)
    ones = jnp.ones((s, HEAD_DIM - ROT_DIM), F32)
    cos64 = jnp.concatenate([cos8, cos8, ones], axis=1)
    sin64 = jnp.concatenate([-sin8, sin8, 0.0 * ones], axis=1)
    return jnp.tile(cos64, (1, 128 // HEAD_DIM)), jnp.tile(sin64, (1, 128 // HEAD_DIM))


def kernel(x_prompt, x_sample, c_prompt, c_sample, w_ada, b_ada, g_norm1, w_in, g_q, g_k, lambda_q1, lambda_k1, lambda_q2, lambda_k2, g_subln, w_dw, b_dw, g_conv_ln, b_conv_ln, w_out, g_norm2, w_router_group, b_router_group, w_router_expert, b_router_expert, w_gate_up, w_down):
    depth = w_ada.shape[0]
    d = x_prompt.shape[-1]
    qw = N_HEADS * 2 * HEAD_DIM
    aw = N_HEADS * V_DIM
    elo_tab, ehi_tab = _class_tables()
    seg = jnp.asarray(np.kron(np.eye(qw // HEAD_DIM), np.ones((HEAD_DIM, HEAD_DIM))), BF16)
    tri = jnp.asarray(np.tril(np.ones((ROUTE_TILE, ROUTE_TILE)), -1), BF16)
    groups = [x_prompt, x_sample]
    conds = [c_prompt, c_sample]
    nb = [c.shape[0] for c in conds]

    for l in range(depth):
        lam_init = 0.8 - 0.6 * math.exp(-0.3 * l)
        lam = (jnp.exp(jnp.sum(lambda_q1[l] * lambda_k1[l])) - jnp.exp(jnp.sum(lambda_q2[l] * lambda_k2[l]))
               + lam_init).reshape(1).astype(F32)
        mod = _ada(jnp.concatenate(conds, axis=0), w_ada[l], b_ada[l])
        w_in_b = w_in[l].astype(BF16)
        w_o_b = w_out[l][:aw].astype(BF16)
        w_c_b = w_out[l][aw:].astype(BF16)
        w_gu_b = w_gate_up[l].astype(BF16)
        w_d_b = w_down[l].astype(BF16)
        gq = jnp.tile(g_q[l], qw // HEAD_DIM).reshape(1, qw)
        gk = jnp.tile(g_k[l], qw // HEAD_DIM).reshape(1, qw)
        n_r = N_GROUPS + N_GROUPS * EXPERTS_PER_GROUP
        wr = jnp.zeros((d, 128), F32).at[:, :N_GROUPS].set(w_router_group[l]).at[:, N_GROUPS:n_r].set(
            w_router_expert[l]).astype(BF16)
        br = jnp.zeros((1, 128), F32).at[0, :N_GROUPS].set(b_router_group[l]).at[0, N_GROUPS:n_r].set(
            b_router_expert[l])

        x1s, h2xs, counts, g2s = [], [], [], []
        row0 = 0
        for x, n in zip(groups, nb):
            b, s, _ = x.shape
            m = mod[row0:row0 + n].reshape(n, 1, 6 * d)
            row0 += n
            sh1, sc1, g1, sh2, sc2, g2 = [m[:, :, i * d:(i + 1) * d] for i in range(6)]
            cos_t, sin_t = _rope_tables(s)
            q, k, vt, z = _proj(x, sh1, sc1, g_norm1[l].reshape(1, d), w_in_b, seg, gq, gk, cos_t, sin_t)
            o = _attention(lam, q, k, vt, g_subln[l].reshape(V_DIM, 1), 1.0 - lam_init)
            cv = _conv(z, w_dw[l], b_dw[l].reshape(1, -1), g_conv_ln[l].reshape(1, -1), b_conv_ln[l].reshape(1, -1))
            x1, h2x, cnt = _outproj_route(o, cv, x, g1, w_o_b, w_c_b, sh2, sc2, g_norm2[l].reshape(1, d), wr, br, tri)
            x1s.append(x1)
            h2xs.append(h2x.reshape(b * s, d + INFO_W))
            counts.append(cnt[0, :N_CLASSES].astype(jnp.int32))
            g2s.append(g2)

        t_all = sum(h.shape[0] for h in h2xs)
        n_rows = t_all + N_CLASSES * MOE_BLOCK
        n_blk = n_rows // MOE_BLOCK
        tot = sum(counts)
        padded = ((tot + MOE_BLOCK - 1) // MOE_BLOCK) * MOE_BLOCK
        pad_end = jnp.cumsum(padded)
        pad_start = pad_end - padded
        dests, offs = [], jnp.zeros_like(tot)
        for h2x, cnt in zip(h2xs, counts):
            start_col = jnp.zeros((INFO_W, 1), F32).at[:N_CLASSES, 0].set((pad_start + offs).astype(F32))
            dests.append(_dest(h2x, start_col))
            offs = offs + cnt
        blk_cls = jnp.minimum(jnp.searchsorted(pad_end, jnp.arange(n_blk, dtype=jnp.int32) * MOE_BLOCK, side='right'),
                              N_CLASSES - 1)
        blk_elo = jnp.asarray(elo_tab)[blk_cls]
        blk_ehi = jnp.asarray(ehi_tab)[blk_cls]
        n_used = (pad_end[-1:] // MOE_BLOCK).astype(jnp.int32)

        xs = jnp.zeros((n_rows, d + INFO_W), F32)
        for h2x, dest in zip(h2xs, dests):
            xs = _dispatch(dest, h2x, xs)
        ys = _moe(blk_elo, blk_ehi, n_used, xs, w_gu_b, w_d_b)
        outs = []
        for x1, dest, g2 in zip(x1s, dests, g2s):
            b, s, _ = x1.shape
            outs.append(_combine(dest.reshape(b, s // ROW_TILE, 1, ROW_TILE), x1, g2, ys))
        groups = outs
    return tuple(groups)
```

```python
import functools
import math

import numpy as np
import jax
import jax.numpy as jnp
from jax import lax
from jax.experimental import pallas as pl
from jax.experimental.pallas import tpu as pltpu

F32 = jnp.float32
BF16 = jnp.bfloat16

N_HEADS = 4
HEAD_DIM = 64
V_DIM = 128
VT_ROWS = V_DIM + 16
ROT_DIM = 16
ROPE_THETA = 500000.0
CONV_KERNEL = 31
CONV_PAD = CONV_KERNEL // 2
N_GROUPS = 4
EXPERTS_PER_GROUP = 8
N_PAIRS = EXPERTS_PER_GROUP * (EXPERTS_PER_GROUP - 1) // 2
N_CLASSES = N_GROUPS * N_PAIRS
EPS = 1e-6
LOG2E = 1.4426950408889634
NEG_BIG = -1e30

VMEM_LIMIT_BYTES = 58 * 1024 * 1024
TOK_TILE = 512
INFO_W = 128
Q_TILE = 256
KEY_BLOCKS_PER_LOOP = 16
ROUTE_TILE = 512
CONV_CHUNK = 64
HALO = 16
CONV_SHIFT_CHUNK = (ROUTE_TILE + 2 * HALO) // 4
ROW_TILE = 512
DMA_ROWS_PER_ITER = 8
MOE_BLOCK = 256


def _cparams(*sem):
    return pltpu.CompilerParams(dimension_semantics=sem, vmem_limit_bytes=VMEM_LIMIT_BYTES)


def _ada_kernel(c_ref, w_ref, b_ref, o_ref):
    c = c_ref[...]
    s = c * jax.nn.sigmoid(c)
    o_ref[...] = jnp.dot(s.astype(BF16), w_ref[...].astype(BF16), preferred_element_type=F32) + b_ref[...]


def _ada(c_all, w_ada, b_ada):
    nb, d = c_all.shape
    n = w_ada.shape[1]
    tn = n // 4
    return pl.pallas_call(
        _ada_kernel,
        grid=(n // tn,),
        in_specs=[pl.BlockSpec((nb, d), lambda j: (0, 0)),
                  pl.BlockSpec((d, tn), lambda j: (0, j)),
                  pl.BlockSpec((1, tn), lambda j: (0, j))],
        out_specs=pl.BlockSpec((nb, tn), lambda j: (0, j)),
        out_shape=jax.ShapeDtypeStruct((nb, n), F32),
        compiler_params=_cparams("arbitrary"),
        name="ada",
    )(c_all, w_ada, b_ada.reshape(1, n))


def _proj_kernel(x_ref, sh_ref, sc_ref, g1_ref, w_ref, seg_ref, gq_ref, gk_ref, cos_ref, sin_ref,
                 q_ref, k_ref, vt_ref, z_ref, *, q_scale):
    x = x_ref[...]
    ms = jnp.mean(x * x, axis=-1, keepdims=True)
    xn = x * lax.rsqrt(ms + EPS) * g1_ref[...]
    h = xn * (1.0 + sc_ref[...]) + sh_ref[...]
    proj = jnp.dot(h.astype(BF16), w_ref[...], preferred_element_type=F32)
    qw = N_HEADS * 2 * HEAD_DIM
    cos = cos_ref[...]
    sin = sin_ref[...]
    lane = lax.broadcasted_iota(jnp.int32, cos.shape, 1)
    first_half = jnp.bitwise_and(lane, HEAD_DIM - 1) < (ROT_DIM // 2)

    def norm_rot(t, g):
        ss = jnp.dot((t * t).astype(BF16), seg_ref[...], preferred_element_type=F32)
        tn = t * lax.rsqrt(ss * (1.0 / HEAD_DIM) + EPS) * g
        outs = []
        for j in range(qw // 128):
            tj = tn[:, j * 128:(j + 1) * 128]
            up = pltpu.roll(tj, 128 - ROT_DIM // 2, axis=1)
            dn = pltpu.roll(tj, ROT_DIM // 2, axis=1)
            outs.append(tj * cos + jnp.where(first_half, up, dn) * sin)
        return jnp.concatenate(outs, axis=1)

    q = norm_rot(proj[:, 0:qw], gq_ref[...])
    k = norm_rot(proj[:, qw:2 * qw], gk_ref[...])
    q_ref[...] = (q * q_scale).astype(BF16)
    k_ref[...] = k.astype(BF16)
    aw = N_HEADS * V_DIM
    v = proj[:, 2 * qw:2 * qw + aw]
    vt = v.T
    extra = (lax.broadcasted_iota(jnp.int32, (VT_ROWS - V_DIM, vt.shape[1]), 0) == 0).astype(F32)
    pieces = []
    for hh in range(N_HEADS):
        pieces += [vt[hh * V_DIM:(hh + 1) * V_DIM], extra]
    vt_ref[...] = jnp.concatenate(pieces, axis=0).astype(BF16)
    a = proj[:, 2 * qw + aw:2 * qw + 2 * aw]
    gate = proj[:, 2 * qw + 2 * aw:2 * qw + 3 * aw]
    z_ref[...] = a * jax.nn.sigmoid(gate)


def _proj(x, sh1, sc1, g1n, w_in, seg, gq, gk, cos_t, sin_t):
    b, s, d = x.shape
    tm = TOK_TILE
    qw = N_HEADS * 2 * HEAD_DIM
    aw = N_HEADS * V_DIM
    cw = w_in.shape[1] - 2 * qw - aw
    assert cw == 2 * aw and s % tm == 0
    row = lambda bb, i: (bb, 0, 0)
    const2 = lambda bb, i: (0, 0)
    tile = lambda bb, i: (bb, i, 0)
    kern = functools.partial(_proj_kernel, q_scale=HEAD_DIM ** -0.5 * LOG2E)
    return pl.pallas_call(
        kern,
        grid=(b, s // tm),
        in_specs=[pl.BlockSpec((None, tm, d), tile),
                  pl.BlockSpec((None, 1, d), row),
                  pl.BlockSpec((None, 1, d), row),
                  pl.BlockSpec((1, d), const2),
                  pl.BlockSpec(w_in.shape, const2),
                  pl.BlockSpec(seg.shape, const2),
                  pl.BlockSpec((1, qw), const2),
                  pl.BlockSpec((1, qw), const2),
                  pl.BlockSpec((tm, 128), lambda bb, i: (i, 0)),
                  pl.BlockSpec((tm, 128), lambda bb, i: (i, 0))],
        out_specs=[pl.BlockSpec((None, tm, qw), tile),
                   pl.BlockSpec((None, tm, qw), tile),
                   pl.BlockSpec((None, None, N_HEADS * VT_ROWS, tm), lambda bb, i: (bb, i, 0, 0)),
                   pl.BlockSpec((None, tm, aw), tile)],
        out_shape=[jax.ShapeDtypeStruct((b, s, qw), BF16),
                   jax.ShapeDtypeStruct((b, s, qw), BF16),
                   jax.ShapeDtypeStruct((b, s // tm, N_HEADS * VT_ROWS, tm), BF16),
                   jax.ShapeDtypeStruct((b, s, aw), F32)],
        compiler_params=_cparams("parallel", "parallel"),
        name="proj",
    )(x, sh1, sc1, g1n, w_in, seg, gq, gk, cos_t, sin_t)


def _attn_kernel(lam_ref, q_ref, k_ref, vt_ref, gs_ref, o_ref, s_ref, mb_ref, m_ref, acc_ref,
                 *, sk, mq, unroll, out_scale):
    nkb = vt_ref.shape[0]
    nq = q_ref.shape[0] // mq
    npairs = nq * nkb
    kb_bits = nkb.bit_length() - 1

    def init(qi, carry):
        m_ref[qi] = jnp.full(m_ref.shape[1:], NEG_BIG, F32)
        acc_ref[qi] = jnp.zeros(acc_ref.shape[1:], F32)
        return carry

    lax.fori_loop(0, nq, init, 0)

    def scores(t, slot):
        qi = lax.shift_right_logical(t, kb_bits)
        kb = jnp.bitwise_and(t, nkb - 1)
        q = q_ref[pl.ds(pl.multiple_of(qi * mq, mq), mq), :]
        lane = lax.broadcasted_iota(jnp.int32, q.shape, 1)
        zero = jnp.zeros_like(q)
        kblk = k_ref[pl.ds(pl.multiple_of(kb * sk, sk), sk), :]
        for c in range(2):
            qm = jnp.where((lane < HEAD_DIM) if c == 0 else (lane >= HEAD_DIM), q, zero)
            s = lax.dot_general(kblk, qm, (((1,), (1,)), ((), ())), preferred_element_type=F32)
            s_ref[slot, c] = s
            mb_ref[slot, c] = jnp.max(s, axis=0, keepdims=True)

    def accumulate(qi, kb, slot):
        vblk = vt_ref[kb]
        for c in range(2):
            m_old = m_ref[qi, c]
            m_new = jnp.maximum(m_old, mb_ref[slot, c])
            alpha = jnp.exp2(m_old - m_new)
            p = jnp.exp2(s_ref[slot, c] - m_new)
            acc_ref[qi, c] = alpha * acc_ref[qi, c] + jnp.dot(vblk, p.astype(BF16), preferred_element_type=F32)
            m_ref[qi, c] = m_new

    def group(t0, last):
        qi = lax.shift_right_logical(t0, kb_bits)
        kb0 = jnp.bitwise_and(t0, nkb - 1)
        for i in range(unroll):
            if not (last and i == unroll - 1):
                scores(t0 + i + 1, (i + 1) % 2)
            accumulate(qi, kb0 + i, i % 2)

    scores(jnp.int32(0), 0)

    def body(j, carry):
        group(unroll * j, False)
        return carry

    lax.fori_loop(0, npairs // unroll - 1, body, 0)
    group(jnp.int32(npairs - unroll), True)
    lam = lam_ref[0]

    def finish(qi, carry):
        a0 = acc_ref[qi, 0]
        a1 = acc_ref[qi, 1]
        o = (a0[:V_DIM] * (1.0 / a0[V_DIM:V_DIM + 1])
             - lam * (a1[:V_DIM] * (1.0 / a1[V_DIM:V_DIM + 1])))
        ms = jnp.mean(o * o, axis=0, keepdims=True)
        on = o * lax.rsqrt(ms + EPS) * (gs_ref[...] * out_scale)
        o_ref[pl.ds(pl.multiple_of(qi * mq, mq), mq), :] = on.T.astype(BF16)
        return carry

    lax.fori_loop(0, nq, finish, 0)


def _attention(lam, q, k, vt, gsub_col, out_scale):
    b, s, qw = q.shape
    nkb, _, sk = vt.shape[1:]
    aw = N_HEADS * V_DIM
    mq = Q_TILE
    nq = s // mq
    unroll = min(nkb, KEY_BLOCKS_PER_LOOP)
    assert unroll % 2 == 0 and nkb % unroll == 0 and nkb & (nkb - 1) == 0
    kern = functools.partial(_attn_kernel, sk=sk, mq=mq, unroll=unroll, out_scale=out_scale)
    head = lambda bb, h: (bb, 0, h)
    return pl.pallas_call(
        kern,
        grid=(b, N_HEADS),
        in_specs=[pl.BlockSpec(memory_space=pltpu.SMEM),
                  pl.BlockSpec((None, s, 128), head),
                  pl.BlockSpec((None, s, 128), head),
                  pl.BlockSpec((None, nkb, VT_ROWS, sk), lambda bb, h: (bb, 0, h, 0)),
                  pl.BlockSpec((V_DIM, 1), lambda bb, h: (0, 0))],
        out_specs=pl.BlockSpec((None, s, V_DIM), head),
        out_shape=jax.ShapeDtypeStruct((b, s, aw), BF16),
        scratch_shapes=[pltpu.VMEM((2, 2, sk, mq), F32), pltpu.VMEM((2, 2, 1, mq), F32),
                        pltpu.VMEM((nq, 2, 1, mq), F32), pltpu.VMEM((nq, 2, VT_ROWS, mq), F32)],
        compiler_params=_cparams("parallel", "parallel"),
        name="attn",
    )(lam, q, k, vt, gsub_col)


def _conv_tile(t, prev_ref, cur_ref, next_ref, w_ref, b_ref, g_ref, beta_ref, o_ref, ext_ref, sh_ref):
    tc, c = cur_ref.shape
    n_ext = tc + 2 * HALO
    keep_prev = (t > 0).astype(F32)
    keep_next = (t < pl.num_programs(1) - 1).astype(F32)
    ext_ref[0:HALO, :] = prev_ref[...] * keep_prev
    ext_ref[HALO:HALO + tc, :] = cur_ref[...]
    ext_ref[HALO + tc:n_ext, :] = next_ref[...] * keep_next
    ext_ref[n_ext:n_ext + 8, :] = jnp.zeros((8, c), F32)
    for o in range(1, 8):
        for e0 in range(0, n_ext, CONV_SHIFT_CHUNK):
            sh_ref[o - 1, e0:e0 + CONV_SHIFT_CHUNK, :] = ext_ref[e0 + o:e0 + o + CONV_SHIFT_CHUNK, :]
    w = w_ref[...]
    base = HALO - CONV_PAD
    for r0 in range(0, tc, CONV_CHUNK):
        acc = jnp.broadcast_to(b_ref[...], (CONV_CHUNK, c))
        for j in range(CONV_KERNEL):
            o = (base + j) % 8
            a0 = r0 + base + j - o
            src = ext_ref if o == 0 else sh_ref.at[o - 1]
            acc = acc + w[j:j + 1, :] * src[a0:a0 + CONV_CHUNK, :]
        mu = jnp.mean(acc, axis=-1, keepdims=True)
        cen = acc - mu
        var = jnp.mean(cen * cen, axis=-1, keepdims=True)
        y = cen * lax.rsqrt(var + EPS) * g_ref[...] + beta_ref[...]
        o_ref[r0:r0 + CONV_CHUNK, :] = (y * jax.nn.sigmoid(y)).astype(BF16)


def _mix_route_kernel(zp_ref, z_ref, zn_ref, wdw_ref, bdw_ref, gln_ref, bln_ref,
                      o_ref, x_ref, g1_ref, wo_ref, wc_ref, sh_ref, sc_ref, g2n_ref, wr_ref, br_ref, tri_ref,
                      x1_ref, h2x_ref, cnt_ref, ext_ref, shift_ref, cv_ref, carry_ref):
    first = jnp.logical_and(pl.program_id(0) == 0, pl.program_id(1) == 0)

    @pl.when(first)
    def _():
        carry_ref[...] = jnp.zeros(carry_ref.shape, F32)

    _conv_tile(pl.program_id(1), zp_ref, z_ref, zn_ref, wdw_ref, bdw_ref, gln_ref, bln_ref, cv_ref, ext_ref,
               shift_ref)
    mix = jnp.dot(o_ref[...], wo_ref[...], preferred_element_type=F32)
    mix = mix + jnp.dot(cv_ref[...], wc_ref[...], preferred_element_type=F32)
    x = x_ref[...] + g1_ref[...] * mix
    x1_ref[...] = x
    d = x.shape[1]
    ms = jnp.mean(x * x, axis=-1, keepdims=True)
    h2 = x * lax.rsqrt(ms + EPS) * g2n_ref[...] * (1.0 + sc_ref[...]) + sh_ref[...]
    h2x_ref[:, 0:d] = h2
    lg = jnp.dot(h2.astype(BF16), wr_ref[...], preferred_element_type=F32) + br_ref[...]
    lane = lax.broadcasted_iota(jnp.int32, lg.shape, 1).astype(F32)
    far = jnp.float32(1e9)
    in_g = lane < N_GROUPS
    lgm = jnp.where(in_g, lg, NEG_BIG)
    gmax = jnp.max(lgm, axis=1, keepdims=True)
    gsel = jnp.min(jnp.where(lgm == gmax, lane, far), axis=1, keepdims=True)
    p_group = 1.0 / jnp.sum(jnp.where(in_g, jnp.exp(lg - gmax), 0.0), axis=1, keepdims=True)
    lo_lane = N_GROUPS + gsel * EXPERTS_PER_GROUP
    in_e = jnp.logical_and(lane >= lo_lane, lane < lo_lane + EXPERTS_PER_GROUP)
    le = jnp.where(in_e, lg, NEG_BIG)
    v1 = jnp.max(le, axis=1, keepdims=True)
    l1 = jnp.min(jnp.where(le == v1, lane, far), axis=1, keepdims=True)
    le2 = jnp.where(lane == l1, NEG_BIG, le)
    v2 = jnp.max(le2, axis=1, keepdims=True)
    l2 = jnp.min(jnp.where(le2 == v2, lane, far), axis=1, keepdims=True)
    r = jnp.exp(v2 - v1)
    w1 = p_group / (1.0 + r)
    w2 = w1 * r
    i1 = l1 - lo_lane
    i2 = l2 - lo_lane
    a = jnp.minimum(i1, i2)
    b = jnp.maximum(i1, i2)
    w_lo = jnp.where(i1 < i2, w1, w2)
    w_hi = jnp.where(i1 < i2, w2, w1)
    cls = gsel * N_PAIRS + a * (2 * EXPERTS_PER_GROUP - 1 - a) * 0.5 + (b - a - 1.0)
    onehot = (lane == cls).astype(F32)
    excl = jnp.dot(tri_ref[...], onehot.astype(BF16), preferred_element_type=F32)
    rank = jnp.sum(onehot * (excl + carry_ref[...]), axis=1, keepdims=True)
    carry_ref[...] = carry_ref[...] + jnp.sum(onehot, axis=0, keepdims=True)
    info = jnp.where(lane == 0, cls, jnp.where(lane == 1, rank, jnp.where(lane == 2, w_lo,
                                                                           jnp.where(lane == 3, w_hi, 0.0))))
    h2x_ref[:, d:d + INFO_W] = info
    cnt_ref[...] = jnp.broadcast_to(carry_ref[...], cnt_ref.shape)


def _mix_route(z, w_dw, b_dw, g_ln, b_ln, o, x, g1, w_o, w_c, sh2, sc2, g2n, wr, br, tri):
    b, s, d = x.shape
    tr = ROUTE_TILE
    aw = o.shape[-1]
    c = z.shape[-1]
    nh = tr // HALO
    last = s // HALO - 1
    tile = lambda bb, i: (bb, i, 0)
    row = lambda bb, i: (bb, 0, 0)
    const2 = lambda bb, i: (0, 0)
    return pl.pallas_call(
        _mix_route_kernel,
        grid=(b, s // tr),
        in_specs=[pl.BlockSpec((None, HALO, c), lambda bb, i: (bb, jnp.maximum(i * nh - 1, 0), 0)),
                  pl.BlockSpec((None, tr, c), tile),
                  pl.BlockSpec((None, HALO, c), lambda bb, i: (bb, jnp.minimum((i + 1) * nh, last), 0)),
                  pl.BlockSpec((CONV_KERNEL, c), const2),
                  pl.BlockSpec((1, c), const2), pl.BlockSpec((1, c), const2), pl.BlockSpec((1, c), const2),
                  pl.BlockSpec((None, tr, aw), tile),
                  pl.BlockSpec((None, tr, d), tile), pl.BlockSpec((None, 1, d), row),
                  pl.BlockSpec(w_o.shape, const2), pl.BlockSpec(w_c.shape, const2),
                  pl.BlockSpec((None, 1, d), row), pl.BlockSpec((None, 1, d), row),
                  pl.BlockSpec((1, d), const2), pl.BlockSpec(wr.shape, const2), pl.BlockSpec((1, 128), const2),
                  pl.BlockSpec((tr, tr), const2)],
        out_specs=[pl.BlockSpec((None, tr, d), tile), pl.BlockSpec((None, tr, d + INFO_W), tile),
                   pl.BlockSpec((8, 128), const2)],
        out_shape=[jax.ShapeDtypeStruct((b, s, d), F32), jax.ShapeDtypeStruct((b, s, d + INFO_W), F32),
                   jax.ShapeDtypeStruct((8, 128), F32)],
        scratch_shapes=[pltpu.VMEM((tr + 2 * HALO + 8, c), F32), pltpu.VMEM((7, tr + 2 * HALO, c), F32),
                        pltpu.VMEM((tr, c), BF16), pltpu.VMEM((1, 128), F32)],
        compiler_params=_cparams("arbitrary", "arbitrary"),
        name="mix_route",
    )(z, z, z, w_dw, b_dw, g_ln, b_ln, o, x, g1, w_o, w_c, sh2, sc2, g2n, wr, br, tri)


def _dest_kernel(info_ref, start_ref, dest_ref):
    info_t = info_ref[...].T
    cls = info_t[0:1, :]
    rank = info_t[1:2, :]
    cid = lax.broadcasted_iota(jnp.int32, info_t.shape, 0).astype(F32)
    base = jnp.sum(jnp.where(cid == cls, start_ref[...], 0.0), axis=0, keepdims=True)
    dest_ref[...] = (base + rank).astype(jnp.int32)


def _dest(h2x, start_col):
    t, w = h2x.shape
    tr = ROW_TILE
    return pl.pallas_call(
        _dest_kernel,
        grid=(t // tr,),
        in_specs=[pl.BlockSpec((tr, INFO_W), lambda i: (i, (w - INFO_W) // INFO_W)),
                  pl.BlockSpec((INFO_W, 1), lambda i: (0, 0))],
        out_specs=pl.BlockSpec((None, 1, tr), lambda i: (i, 0, 0)),
        out_shape=jax.ShapeDtypeStruct((t // tr, 1, tr), jnp.int32),
        compiler_params=_cparams("parallel"),
        name="dest",
    )(h2x, start_col)


def _row_copy(src_ref, dst_ref, src_row, dst_row, sem):
    return pltpu.make_async_copy(src_ref.at[pl.ds(src_row, 1), :], dst_ref.at[pl.ds(dst_row, 1), :], sem)


def _start_rows(n, copy_of_row):
    def start(g, c):
        for u in range(DMA_ROWS_PER_ITER):
            copy_of_row(g * DMA_ROWS_PER_ITER + u).start(priority=u % 2)
        return c

    lax.fori_loop(0, n // DMA_ROWS_PER_ITER, start, 0)


def _scatter_rows(dest_ref, h_ref, xs_ref, sem):
    n = h_ref.shape[0]
    _start_rows(n, lambda r: _row_copy(h_ref, xs_ref, r, dest_ref[0, r], sem))

    def wait(r, c):
        _row_copy(h_ref, xs_ref, 0, 0, sem).wait()
        return c

    lax.fori_loop(0, n, wait, 0, unroll=8)


def _dispatch_kernel(dest_ref, h_ref, xs_in, xs_ref, sem):
    del xs_in
    _scatter_rows(dest_ref, h_ref, xs_ref, sem.at[0])


def _dispatch(dest3, h2x, xs):
    t, w = h2x.shape
    tr = ROW_TILE
    return pl.pallas_call(
        _dispatch_kernel,
        grid=(t // tr,),
        in_specs=[pl.BlockSpec((None, 1, tr), lambda i: (i, 0, 0), memory_space=pltpu.SMEM),
                  pl.BlockSpec((tr, w), lambda i: (i, 0)),
                  pl.BlockSpec(memory_space=pl.ANY)],
        out_specs=pl.BlockSpec(memory_space=pl.ANY),
        out_shape=jax.ShapeDtypeStruct(xs.shape, F32),
        scratch_shapes=[pltpu.SemaphoreType.DMA((1,))],
        input_output_aliases={2: 0},
        compiler_params=_cparams("arbitrary"),
        name="dispatch",
    )(dest3, h2x, xs)


def _moe_kernel(elo_ref, ehi_ref, nused_ref, xs_ref, wgu_lo, wd_lo, wgu_hi, wd_hi, ys_ref):
    del elo_ref, ehi_ref
    used = pl.program_id(0) < nused_ref[0]
    d = ys_ref.shape[1]

    @pl.when(used)
    def _():
        x = xs_ref[:, 0:d].astype(BF16)
        info = xs_ref[:, d:d + INFO_W]
        ff = wd_lo.shape[0]

        def expert(wgu, wd):
            gu = jnp.dot(x, wgu[...], preferred_element_type=F32)
            g = gu[:, :ff]
            u = gu[:, ff:]
            act = g * jax.nn.sigmoid(g) * u
            return jnp.dot(act.astype(BF16), wd[...], preferred_element_type=F32)

        ys_ref[...] = info[:, 2:3] * expert(wgu_lo, wd_lo) + info[:, 3:4] * expert(wgu_hi, wd_hi)

    @pl.when(jnp.logical_not(used))
    def _():
        ys_ref[...] = jnp.zeros(ys_ref.shape, F32)


def _moe(blk_elo, blk_ehi, n_used, xs, w_gu, w_d):
    n_rows, w = xs.shape
    d = w - INFO_W
    tb = MOE_BLOCK
    n_blk = n_rows // tb
    ff = w_d.shape[1]
    grid_spec = pltpu.PrefetchScalarGridSpec(
        num_scalar_prefetch=3,
        grid=(n_blk,),
        in_specs=[pl.BlockSpec((tb, w), lambda i, elo, ehi, nu: (jnp.minimum(i, nu[0] - 1), 0)),
                  pl.BlockSpec((None, d, 2 * ff), lambda i, elo, ehi, nu: (elo[i], 0, 0)),
                  pl.BlockSpec((None, ff, d), lambda i, elo, ehi, nu: (elo[i], 0, 0)),
                  pl.BlockSpec((None, d, 2 * ff), lambda i, elo, ehi, nu: (ehi[i], 0, 0)),
                  pl.BlockSpec((None, ff, d), lambda i, elo, ehi, nu: (ehi[i], 0, 0))],
        out_specs=pl.BlockSpec((tb, d), lambda i, elo, ehi, nu: (i, 0)),
    )
    return pl.pallas_call(
        _moe_kernel,
        grid_spec=grid_spec,
        out_shape=jax.ShapeDtypeStruct((n_rows, d), F32),
        compiler_params=_cparams("arbitrary"),
        name="moe",
    )(blk_elo, blk_ehi, n_used, xs, w_gu, w_d, w_gu, w_d)


def _combine_kernel(dest_ref, x1_ref, g2_ref, ys_ref, o_ref, buf_ref, sem):
    n = x1_ref.shape[0]
    _start_rows(n, lambda r: _row_copy(ys_ref, buf_ref, dest_ref[0, r], r, sem.at[0]))

    def wait(r, c):
        _row_copy(ys_ref, buf_ref, 0, 0, sem.at[0]).wait()
        return c

    lax.fori_loop(0, n, wait, 0, unroll=8)
    o_ref[...] = x1_ref[...] + g2_ref[...] * buf_ref[...]


def _combine(dest4, x1, g2, ys):
    b, s, d = x1.shape
    tr = ROW_TILE
    return pl.pallas_call(
        _combine_kernel,
        grid=(b, s // tr),
        in_specs=[pl.BlockSpec((None, None, 1, tr), lambda bb, i: (bb, i, 0, 0), memory_space=pltpu.SMEM),
                  pl.BlockSpec((None, tr, d), lambda bb, i: (bb, i, 0)),
                  pl.BlockSpec((None, 1, d), lambda bb, i: (bb, 0, 0)),
                  pl.BlockSpec(memory_space=pl.ANY)],
        out_specs=pl.BlockSpec((None, tr, d), lambda bb, i: (bb, i, 0)),
        out_shape=jax.ShapeDtypeStruct((b, s, d), F32),
        scratch_shapes=[pltpu.VMEM((tr, d), F32), pltpu.SemaphoreType.DMA((1,))],
        compiler_params=_cparams("arbitrary", "arbitrary"),
        name="combine",
    )(dest4, x1, g2, ys)


def _class_tables():
    elo, ehi = [], []
    for g in range(N_GROUPS):
        for a in range(EXPERTS_PER_GROUP):
            for b in range(a + 1, EXPERTS_PER_GROUP):
                elo.append(g * EXPERTS_PER_GROUP + a)
                ehi.append(g * EXPERTS_PER_GROUP + b)
    return np.asarray(elo, np.int32), np.asarray(ehi, np.int32)


def _rope_tables(s):
    inv_freq = ROPE_THETA ** (-jnp.arange(0, ROT_DIM, 2, dtype=F32) / ROT_DIM)
    ang = jnp.arange(s, dtype=F32)[:, None] * inv_freq[None, :]
    cos8, sin8 = jnp.cos(ang), jnp.sin(ang)
    ones = jnp.ones((s, HEAD_DIM - ROT_DIM), F32)
    cos64 = jnp.concatenate([cos8, cos8, ones], axis=1)
    sin64 = jnp.concatenate([-sin8, sin8, 0.0 * ones], axis=1)
    return jnp.tile(cos64, (1, 128 // HEAD_DIM)), jnp.tile(sin64, (1, 128 // HEAD_DIM))


def kernel(x_prompt, x_sample, c_prompt, c_sample, w_ada, b_ada, g_norm1, w_in, g_q, g_k, lambda_q1, lambda_k1, lambda_q2, lambda_k2, g_subln, w_dw, b_dw, g_conv_ln, b_conv_ln, w_out, g_norm2, w_router_group, b_router_group, w_router_expert, b_router_expert, w_gate_up, w_down):
    depth = w_ada.shape[0]
    d = x_prompt.shape[-1]
    qw = N_HEADS * 2 * HEAD_DIM
    aw = N_HEADS * V_DIM
    elo_tab, ehi_tab = _class_tables()
    seg = jnp.asarray(np.kron(np.eye(qw // HEAD_DIM), np.ones((HEAD_DIM, HEAD_DIM))), BF16)
    tri = jnp.asarray(np.tril(np.ones((ROUTE_TILE, ROUTE_TILE)), -1), BF16)
    groups = [x_prompt, x_sample]
    conds = [c_prompt, c_sample]
    nb = [c.shape[0] for c in conds]

    for l in range(depth):
        lam_init = 0.8 - 0.6 * math.exp(-0.3 * l)
        lam = (jnp.exp(jnp.sum(lambda_q1[l] * lambda_k1[l])) - jnp.exp(jnp.sum(lambda_q2[l] * lambda_k2[l]))
               + lam_init).reshape(1).astype(F32)
        mod = _ada(jnp.concatenate(conds, axis=0), w_ada[l], b_ada[l])
        w_in_b = w_in[l].astype(BF16)
        w_o_b = w_out[l][:aw].astype(BF16)
        w_c_b = w_out[l][aw:].astype(BF16)
        w_gu_b = w_gate_up[l].astype(BF16)
        w_d_b = w_down[l].astype(BF16)
        gq = jnp.tile(g_q[l], qw // HEAD_DIM).reshape(1, qw)
        gk = jnp.tile(g_k[l], qw // HEAD_DIM).reshape(1, qw)
        n_r = N_GROUPS + N_GROUPS * EXPERTS_PER_GROUP
        wr = jnp.zeros((d, 128), F32).at[:, :N_GROUPS].set(w_router_group[l]).at[:, N_GROUPS:n_r].set(
            w_router_expert[l]).astype(BF16)
        br = jnp.zeros((1, 128), F32).at[0, :N_GROUPS].set(b_router_group[l]).at[0, N_GROUPS:n_r].set(
            b_router_expert[l])

        x1s, h2xs, counts, g2s = [], [], [], []
        row0 = 0
        for x, n in zip(groups, nb):
            b, s, _ = x.shape
            m = mod[row0:row0 + n].reshape(n, 1, 6 * d)
            row0 += n
            sh1, sc1, g1, sh2, sc2, g2 = [m[:, :, i * d:(i + 1) * d] for i in range(6)]
            cos_t, sin_t = _rope_tables(s)
            q, k, vt, z = _proj(x, sh1, sc1, g_norm1[l].reshape(1, d), w_in_b, seg, gq, gk, cos_t, sin_t)
            o = _attention(lam, q, k, vt, g_subln[l].reshape(V_DIM, 1), 1.0 - lam_init)
            x1, h2x, cnt = _mix_route(z, w_dw[l], b_dw[l].reshape(1, -1), g_conv_ln[l].reshape(1, -1),
                                      b_conv_ln[l].reshape(1, -1), o, x, g1, w_o_b, w_c_b, sh2, sc2,
                                      g_norm2[l].reshape(1, d), wr, br, tri)
            x1s.append(x1)
            h2xs.append(h2x.reshape(b * s, d + INFO_W))
            counts.append(cnt[0, :N_CLASSES].astype(jnp.int32))
            g2s.append(g2)

        t_all = sum(h.shape[0] for h in h2xs)
        n_rows = t_all + N_CLASSES * MOE_BLOCK
        n_blk = n_rows // MOE_BLOCK
        tot = sum(counts)
        padded = ((tot + MOE_BLOCK - 1) // MOE_BLOCK) * MOE_BLOCK
        pad_end = jnp.cumsum(padded)
        pad_start = pad_end - padded
        dests, offs = [], jnp.zeros_like(tot)
        for h2x, cnt in zip(h2xs, counts):
            start_col = jnp.zeros((INFO_W, 1), F32).at[:N_CLASSES, 0].set((pad_start + offs).astype(F32))
            dests.append(_dest(h2x, start_col))
            offs = offs + cnt
        blk_cls = jnp.minimum(jnp.searchsorted(pad_end, jnp.arange(n_blk, dtype=jnp.int32) * MOE_BLOCK, side='right'),
                              N_CLASSES - 1)
        blk_elo = jnp.asarray(elo_tab)[blk_cls]
        blk_ehi = jnp.asarray(ehi_tab)[blk_cls]
        n_used = (pad_end[-1:] // MOE_BLOCK).astype(jnp.int32)

        xs = jnp.zeros((n_rows, d + INFO_W), F32)
        for h2x, dest in zip(h2xs, dests):
            xs = _dispatch(dest, h2x, xs)
        ys = _moe(blk_elo, blk_ehi, n_used, xs, w_gu_b, w_d_b)
        outs = []
        for x1, dest, g2 in zip(x1s, dests, g2s):
            b, s, _ = x1.shape
            outs.append(_combine(dest.reshape(b, s // ROW_TILE, 1, ROW_TILE), x1, g2, ys))
        groups = outs
    return tuple(groups)
```

```python
import functools
import math

import numpy as np
import jax
import jax.numpy as jnp
from jax import lax
from jax.experimental import pallas as pl
from jax.experimental.pallas import tpu as pltpu

F32 = jnp.float32
BF16 = jnp.bfloat16

N_HEADS = 4
HEAD_DIM = 64
V_DIM = 128
VT_ROWS = V_DIM + 16
ROT_DIM = 16
ROPE_THETA = 500000.0
CONV_KERNEL = 31
CONV_PAD = CONV_KERNEL // 2
N_GROUPS = 4
EXPERTS_PER_GROUP = 8
N_PAIRS = EXPERTS_PER_GROUP * (EXPERTS_PER_GROUP - 1) // 2
N_CLASSES = N_GROUPS * N_PAIRS
EPS = 1e-6
LOG2E = 1.4426950408889634
NEG_BIG = -1e30

VMEM_LIMIT_BYTES = 58 * 1024 * 1024
TOK_TILE = 512
INFO_W = 128
Q_TILE = 256
KEY_BLOCKS_PER_LOOP = 16
ROUTE_TILE = 512
CONV_CHUNK = 64
HALO = 16
CONV_SHIFT_CHUNK = (ROUTE_TILE + 2 * HALO) // 4
ROW_TILE = 1024
DMA_ROWS_PER_ITER = 8
MOE_BLOCK = 256


def _cparams(*sem):
    return pltpu.CompilerParams(dimension_semantics=sem, vmem_limit_bytes=VMEM_LIMIT_BYTES)


def _ada_kernel(c_ref, w_ref, b_ref, o_ref):
    c = c_ref[...]
    s = c * jax.nn.sigmoid(c)
    o_ref[...] = jnp.dot(s.astype(BF16), w_ref[...].astype(BF16), preferred_element_type=F32) + b_ref[...]


def _ada(c_all, w_ada, b_ada):
    nb, d = c_all.shape
    n = w_ada.shape[1]
    tn = n // 4
    return pl.pallas_call(
        _ada_kernel,
        grid=(n // tn,),
        in_specs=[pl.BlockSpec((nb, d), lambda j: (0, 0)),
                  pl.BlockSpec((d, tn), lambda j: (0, j)),
                  pl.BlockSpec((1, tn), lambda j: (0, j))],
        out_specs=pl.BlockSpec((nb, tn), lambda j: (0, j)),
        out_shape=jax.ShapeDtypeStruct((nb, n), F32),
        compiler_params=_cparams("arbitrary"),
        name="ada",
    )(c_all, w_ada, b_ada.reshape(1, n))


def _proj_kernel(x_ref, sh_ref, sc_ref, g1_ref, w_ref, seg_ref, gq_ref, gk_ref, cos_ref, sin_ref,
                 q_ref, k_ref, vt_ref, z_ref, *, q_scale):
    x = x_ref[...]
    ms = jnp.mean(x * x, axis=-1, keepdims=True)
    xn = x * lax.rsqrt(ms + EPS) * g1_ref[...]
    h = xn * (1.0 + sc_ref[...]) + sh_ref[...]
    proj = jnp.dot(h.astype(BF16), w_ref[...], preferred_element_type=F32)
    qw = N_HEADS * 2 * HEAD_DIM
    cos = cos_ref[...]
    sin = sin_ref[...]
    lane = lax.broadcasted_iota(jnp.int32, cos.shape, 1)
    first_half = jnp.bitwise_and(lane, HEAD_DIM - 1) < (ROT_DIM // 2)

    def norm_rot(t, g):
        ss = jnp.dot((t * t).astype(BF16), seg_ref[...], preferred_element_type=F32)
        tn = t * lax.rsqrt(ss * (1.0 / HEAD_DIM) + EPS) * g
        outs = []
        for j in range(qw // 128):
            tj = tn[:, j * 128:(j + 1) * 128]
            up = pltpu.roll(tj, 128 - ROT_DIM // 2, axis=1)
            dn = pltpu.roll(tj, ROT_DIM // 2, axis=1)
            outs.append(tj * cos + jnp.where(first_half, up, dn) * sin)
        return jnp.concatenate(outs, axis=1)

    q = norm_rot(proj[:, 0:qw], gq_ref[...])
    k = norm_rot(proj[:, qw:2 * qw], gk_ref[...])
    q_ref[...] = (q * q_scale).astype(BF16)
    k_ref[...] = k.astype(BF16)
    aw = N_HEADS * V_DIM
    v = proj[:, 2 * qw:2 * qw + aw]
    vt = v.T
    extra = (lax.broadcasted_iota(jnp.int32, (VT_ROWS - V_DIM, vt.shape[1]), 0) == 0).astype(F32)
    pieces = []
    for hh in range(N_HEADS):
        pieces += [vt[hh * V_DIM:(hh + 1) * V_DIM], extra]
    vt_ref[...] = jnp.concatenate(pieces, axis=0).astype(BF16)
    a = proj[:, 2 * qw + aw:2 * qw + 2 * aw]
    gate = proj[:, 2 * qw + 2 * aw:2 * qw + 3 * aw]
    z_ref[...] = a * jax.nn.sigmoid(gate)


def _proj(x, sh1, sc1, g1n, w_in, seg, gq, gk, cos_t, sin_t):
    b, s, d = x.shape
    tm = TOK_TILE
    qw = N_HEADS * 2 * HEAD_DIM
    aw = N_HEADS * V_DIM
    cw = w_in.shape[1] - 2 * qw - aw
    assert cw == 2 * aw and s % tm == 0
    row = lambda bb, i: (bb, 0, 0)
    const2 = lambda bb, i: (0, 0)
    tile = lambda bb, i: (bb, i, 0)
    kern = functools.partial(_proj_kernel, q_scale=HEAD_DIM ** -0.5 * LOG2E)
    return pl.pallas_call(
        kern,
        grid=(b, s // tm),
        in_specs=[pl.BlockSpec((None, tm, d), tile),
                  pl.BlockSpec((None, 1, d), row),
                  pl.BlockSpec((None, 1, d), row),
                  pl.BlockSpec((1, d), const2),
                  pl.BlockSpec(w_in.shape, const2),
                  pl.BlockSpec(seg.shape, const2),
                  pl.BlockSpec((1, qw), const2),
                  pl.BlockSpec((1, qw), const2),
                  pl.BlockSpec((tm, 128), lambda bb, i: (i, 0)),
                  pl.BlockSpec((tm, 128), lambda bb, i: (i, 0))],
        out_specs=[pl.BlockSpec((None, tm, qw), tile),
                   pl.BlockSpec((None, tm, qw), tile),
                   pl.BlockSpec((None, None, N_HEADS * VT_ROWS, tm), lambda bb, i: (bb, i, 0, 0)),
                   pl.BlockSpec((None, tm, aw), tile)],
        out_shape=[jax.ShapeDtypeStruct((b, s, qw), BF16),
                   jax.ShapeDtypeStruct((b, s, qw), BF16),
                   jax.ShapeDtypeStruct((b, s // tm, N_HEADS * VT_ROWS, tm), BF16),
                   jax.ShapeDtypeStruct((b, s, aw), F32)],
        compiler_params=_cparams("parallel", "parallel"),
        name="proj",
    )(x, sh1, sc1, g1n, w_in, seg, gq, gk, cos_t, sin_t)


def _attn_kernel(lam_ref, q_ref, k_ref, vt_ref, gs_ref, o_ref, s_ref, mb_ref, m_ref, acc_ref,
                 *, sk, mq, unroll, out_scale):
    nkb = vt_ref.shape[0]
    nq = q_ref.shape[0] // mq
    npairs = nq * nkb
    kb_bits = nkb.bit_length() - 1

    def init(qi, carry):
        m_ref[qi] = jnp.full(m_ref.shape[1:], NEG_BIG, F32)
        acc_ref[qi] = jnp.zeros(acc_ref.shape[1:], F32)
        return carry

    lax.fori_loop(0, nq, init, 0)

    def scores(t, slot):
        qi = lax.shift_right_logical(t, kb_bits)
        kb = jnp.bitwise_and(t, nkb - 1)
        q = q_ref[pl.ds(pl.multiple_of(qi * mq, mq), mq), :]
        lane = lax.broadcasted_iota(jnp.int32, q.shape, 1)
        zero = jnp.zeros_like(q)
        kblk = k_ref[pl.ds(pl.multiple_of(kb * sk, sk), sk), :]
        for c in range(2):
            qm = jnp.where((lane < HEAD_DIM) if c == 0 else (lane >= HEAD_DIM), q, zero)
            s = lax.dot_general(kblk, qm, (((1,), (1,)), ((), ())), preferred_element_type=F32)
            s_ref[slot, c] = s
            mb_ref[slot, c] = jnp.max(s, axis=0, keepdims=True)

    def accumulate(qi, kb, slot):
        vblk = vt_ref[kb]
        for c in range(2):
            m_old = m_ref[qi, c]
            m_new = jnp.maximum(m_old, mb_ref[slot, c])
            alpha = jnp.exp2(m_old - m_new)
            p = jnp.exp2(s_ref[slot, c] - m_new)
            acc_ref[qi, c] = alpha * acc_ref[qi, c] + jnp.dot(vblk, p.astype(BF16), preferred_element_type=F32)
            m_ref[qi, c] = m_new

    def group(t0, last):
        for i in range(unroll):
            if not (last and i == unroll - 1):
                scores(t0 + i + 1, (i + 1) % 2)
            t = t0 + i
            accumulate(lax.shift_right_logical(t, kb_bits), jnp.bitwise_and(t, nkb - 1), i % 2)

    scores(jnp.int32(0), 0)

    def body(j, carry):
        group(unroll * j, False)
        return carry

    lax.fori_loop(0, npairs // unroll - 1, body, 0)
    group(jnp.int32(npairs - unroll), True)
    lam = lam_ref[0]

    def finish(qi, carry):
        a0 = acc_ref[qi, 0]
        a1 = acc_ref[qi, 1]
        o = (a0[:V_DIM] * (1.0 / a0[V_DIM:V_DIM + 1])
             - lam * (a1[:V_DIM] * (1.0 / a1[V_DIM:V_DIM + 1])))
        ms = jnp.mean(o * o, axis=0, keepdims=True)
        on = o * lax.rsqrt(ms + EPS) * (gs_ref[...] * out_scale)
        o_ref[pl.ds(pl.multiple_of(qi * mq, mq), mq), :] = on.T.astype(BF16)
        return carry

    lax.fori_loop(0, nq, finish, 0)


def _attention(lam, q, k, vt, gsub_col, out_scale):
    b, s, qw = q.shape
    nkb, _, sk = vt.shape[1:]
    aw = N_HEADS * V_DIM
    mq = Q_TILE
    nq = s // mq
    unroll = min(nq * nkb, KEY_BLOCKS_PER_LOOP)
    assert unroll % 2 == 0 and (nq * nkb) % unroll == 0 and nkb & (nkb - 1) == 0
    kern = functools.partial(_attn_kernel, sk=sk, mq=mq, unroll=unroll, out_scale=out_scale)
    head = lambda bb, h: (bb, 0, h)
    return pl.pallas_call(
        kern,
        grid=(b, N_HEADS),
        in_specs=[pl.BlockSpec(memory_space=pltpu.SMEM),
                  pl.BlockSpec((None, s, 128), head),
                  pl.BlockSpec((None, s, 128), head),
                  pl.BlockSpec((None, nkb, VT_ROWS, sk), lambda bb, h: (bb, 0, h, 0)),
                  pl.BlockSpec((V_DIM, 1), lambda bb, h: (0, 0))],
        out_specs=pl.BlockSpec((None, s, V_DIM), head),
        out_shape=jax.ShapeDtypeStruct((b, s, aw), BF16),
        scratch_shapes=[pltpu.VMEM((2, 2, sk, mq), F32), pltpu.VMEM((2, 2, 1, mq), F32),
                        pltpu.VMEM((nq, 2, 1, mq), F32), pltpu.VMEM((nq, 2, VT_ROWS, mq), F32)],
        compiler_params=_cparams("parallel", "parallel"),
        name="attn",
    )(lam, q, k, vt, gsub_col)


def _conv_tile(t, prev_ref, cur_ref, next_ref, w_ref, b_ref, g_ref, beta_ref, o_ref, ext_ref, sh_ref):
    tc, c = cur_ref.shape
    n_ext = tc + 2 * HALO
    keep_prev = (t > 0).astype(F32)
    keep_next = (t < pl.num_programs(1) - 1).astype(F32)
    ext_ref[0:HALO, :] = prev_ref[...] * keep_prev
    ext_ref[HALO:HALO + tc, :] = cur_ref[...]
    ext_ref[HALO + tc:n_ext, :] = next_ref[...] * keep_next
    ext_ref[n_ext:n_ext + 8, :] = jnp.zeros((8, c), F32)
    for o in range(1, 8):
        for e0 in range(0, n_ext, CONV_SHIFT_CHUNK):
            sh_ref[o - 1, e0:e0 + CONV_SHIFT_CHUNK, :] = ext_ref[e0 + o:e0 + o + CONV_SHIFT_CHUNK, :]
    w = w_ref[...]
    base = HALO - CONV_PAD
    for r0 in range(0, tc, CONV_CHUNK):
        acc = jnp.broadcast_to(b_ref[...], (CONV_CHUNK, c))
        for j in range(CONV_KERNEL):
            o = (base + j) % 8
            a0 = r0 + base + j - o
            src = ext_ref if o == 0 else sh_ref.at[o - 1]
            acc = acc + w[j:j + 1, :] * src[a0:a0 + CONV_CHUNK, :]
        mu = jnp.mean(acc, axis=-1, keepdims=True)
        cen = acc - mu
        var = jnp.mean(cen * cen, axis=-1, keepdims=True)
        y = cen * lax.rsqrt(var + EPS) * g_ref[...] + beta_ref[...]
        o_ref[r0:r0 + CONV_CHUNK, :] = (y * jax.nn.sigmoid(y)).astype(BF16)


def _mix_route_kernel(zp_ref, z_ref, zn_ref, wdw_ref, bdw_ref, gln_ref, bln_ref,
                      o_ref, x_ref, g1_ref, wo_ref, wc_ref, sh_ref, sc_ref, g2n_ref, wr_ref, br_ref, tri_ref,
                      x1_ref, h2x_ref, cnt_ref, ext_ref, shift_ref, cv_ref, carry_ref):
    first = jnp.logical_and(pl.program_id(0) == 0, pl.program_id(1) == 0)

    @pl.when(first)
    def _():
        carry_ref[...] = jnp.zeros(carry_ref.shape, F32)

    _conv_tile(pl.program_id(1), zp_ref, z_ref, zn_ref, wdw_ref, bdw_ref, gln_ref, bln_ref, cv_ref, ext_ref,
               shift_ref)
    mix = jnp.dot(o_ref[...], wo_ref[...], preferred_element_type=F32)
    mix = mix + jnp.dot(cv_ref[...], wc_ref[...], preferred_element_type=F32)
    x = x_ref[...] + g1_ref[...] * mix
    x1_ref[...] = x
    d = x.shape[1]
    ms = jnp.mean(x * x, axis=-1, keepdims=True)
    h2 = x * lax.rsqrt(ms + EPS) * g2n_ref[...] * (1.0 + sc_ref[...]) + sh_ref[...]
    h2x_ref[:, 0:d] = h2
    lg = jnp.dot(h2.astype(BF16), wr_ref[...], preferred_element_type=F32) + br_ref[...]
    lane = lax.broadcasted_iota(jnp.int32, lg.shape, 1).astype(F32)
    far = jnp.float32(1e9)
    in_g = lane < N_GROUPS
    lgm = jnp.where(in_g, lg, NEG_BIG)
    gmax = jnp.max(lgm, axis=1, keepdims=True)
    gsel = jnp.min(jnp.where(lgm == gmax, lane, far), axis=1, keepdims=True)
    p_group = 1.0 / jnp.sum(jnp.where(in_g, jnp.exp(lg - gmax), 0.0), axis=1, keepdims=True)
    lo_lane = N_GROUPS + gsel * EXPERTS_PER_GROUP
    in_e = jnp.logical_and(lane >= lo_lane, lane < lo_lane + EXPERTS_PER_GROUP)
    le = jnp.where(in_e, lg, NEG_BIG)
    v1 = jnp.max(le, axis=1, keepdims=True)
    l1 = jnp.min(jnp.where(le == v1, lane, far), axis=1, keepdims=True)
    le2 = jnp.where(lane == l1, NEG_BIG, le)
    v2 = jnp.max(le2, axis=1, keepdims=True)
    l2 = jnp.min(jnp.where(le2 == v2, lane, far), axis=1, keepdims=True)
    r = jnp.exp(v2 - v1)
    w1 = p_group / (1.0 + r)
    w2 = w1 * r
    i1 = l1 - lo_lane
    i2 = l2 - lo_lane
    a = jnp.minimum(i1, i2)
    b = jnp.maximum(i1, i2)
    w_lo = jnp.where(i1 < i2, w1, w2)
    w_hi = jnp.where(i1 < i2, w2, w1)
    cls = gsel * N_PAIRS + a * (2 * EXPERTS_PER_GROUP - 1 - a) * 0.5 + (b - a - 1.0)
    onehot = (lane == cls).astype(F32)
    excl = jnp.dot(tri_ref[...], onehot.astype(BF16), preferred_element_type=F32)
    rank = jnp.sum(onehot * (excl + carry_ref[...]), axis=1, keepdims=True)
    carry_ref[...] = carry_ref[...] + jnp.sum(onehot, axis=0, keepdims=True)
    info = jnp.where(lane == 0, cls, jnp.where(lane == 1, rank, jnp.where(lane == 2, w_lo,
                                                                           jnp.where(lane == 3, w_hi, 0.0))))
    h2x_ref[:, d:d + INFO_W] = info
    cnt_ref[...] = jnp.broadcast_to(carry_ref[...], cnt_ref.shape)


def _mix_route(z, w_dw, b_dw, g_ln, b_ln, o, x, g1, w_o, w_c, sh2, sc2, g2n, wr, br, tri):
    b, s, d = x.shape
    tr = ROUTE_TILE
    aw = o.shape[-1]
    c = z.shape[-1]
    nh = tr // HALO
    last = s // HALO - 1
    tile = lambda bb, i: (bb, i, 0)
    row = lambda bb, i: (bb, 0, 0)
    const2 = lambda bb, i: (0, 0)
    return pl.pallas_call(
        _mix_route_kernel,
        grid=(b, s // tr),
        in_specs=[pl.BlockSpec((None, HALO, c), lambda bb, i: (bb, jnp.maximum(i * nh - 1, 0), 0)),
                  pl.BlockSpec((None, tr, c), tile),
                  pl.BlockSpec((None, HALO, c), lambda bb, i: (bb, jnp.minimum((i + 1) * nh, last), 0)),
                  pl.BlockSpec((CONV_KERNEL, c), const2),
                  pl.BlockSpec((1, c), const2), pl.BlockSpec((1, c), const2), pl.BlockSpec((1, c), const2),
                  pl.BlockSpec((None, tr, aw), tile),
                  pl.BlockSpec((None, tr, d), tile), pl.BlockSpec((None, 1, d), row),
                  pl.BlockSpec(w_o.shape, const2), pl.BlockSpec(w_c.shape, const2),
                  pl.BlockSpec((None, 1, d), row), pl.BlockSpec((None, 1, d), row),
                  pl.BlockSpec((1, d), const2), pl.BlockSpec(wr.shape, const2), pl.BlockSpec((1, 128), const2),
                  pl.BlockSpec((tr, tr), const2)],
        out_specs=[pl.BlockSpec((None, tr, d), tile), pl.BlockSpec((None, tr, d + INFO_W), tile),
                   pl.BlockSpec((8, 128), const2)],
        out_shape=[jax.ShapeDtypeStruct((b, s, d), F32), jax.ShapeDtypeStruct((b, s, d + INFO_W), F32),
                   jax.ShapeDtypeStruct((8, 128), F32)],
        scratch_shapes=[pltpu.VMEM((tr + 2 * HALO + 8, c), F32), pltpu.VMEM((7, tr + 2 * HALO, c), F32),
                        pltpu.VMEM((tr, c), BF16), pltpu.VMEM((1, 128), F32)],
        compiler_params=_cparams("arbitrary", "arbitrary"),
        name="mix_route",
    )(z, z, z, w_dw, b_dw, g_ln, b_ln, o, x, g1, w_o, w_c, sh2, sc2, g2n, wr, br, tri)


def _dest_kernel(info_ref, start_ref, dest_ref):
    info_t = info_ref[...].T
    cls = info_t[0:1, :]
    rank = info_t[1:2, :]
    cid = lax.broadcasted_iota(jnp.int32, info_t.shape, 0).astype(F32)
    base = jnp.sum(jnp.where(cid == cls, start_ref[...], 0.0), axis=0, keepdims=True)
    dest_ref[...] = (base + rank).astype(jnp.int32)


def _dest(h2x, start_col):
    t, w = h2x.shape
    tr = ROW_TILE
    return pl.pallas_call(
        _dest_kernel,
        grid=(t // tr,),
        in_specs=[pl.BlockSpec((tr, INFO_W), lambda i: (i, (w - INFO_W) // INFO_W)),
                  pl.BlockSpec((INFO_W, 1), lambda i: (0, 0))],
        out_specs=pl.BlockSpec((None, 1, tr), lambda i: (i, 0, 0)),
        out_shape=jax.ShapeDtypeStruct((t // tr, 1, tr), jnp.int32),
        compiler_params=_cparams("parallel"),
        name="dest",
    )(h2x, start_col)


def _row_copy(src_ref, dst_ref, src_row, dst_row, sem):
    return pltpu.make_async_copy(src_ref.at[pl.ds(src_row, 1), :], dst_ref.at[pl.ds(dst_row, 1), :], sem)


def _start_rows(n, copy_of_row):
    def start(g, c):
        for u in range(DMA_ROWS_PER_ITER):
            copy_of_row(g * DMA_ROWS_PER_ITER + u).start(priority=u % 2)
        return c

    lax.fori_loop(0, n // DMA_ROWS_PER_ITER, start, 0)


def _scatter_rows(dest_ref, h_ref, xs_ref, sem):
    n = h_ref.shape[0]
    _start_rows(n, lambda r: _row_copy(h_ref, xs_ref, r, dest_ref[0, r], sem))

    def wait(r, c):
        _row_copy(h_ref, xs_ref, 0, 0, sem).wait()
        return c

    lax.fori_loop(0, n, wait, 0, unroll=8)


def _dispatch_kernel(dest_ref, h_ref, xs_in, xs_ref, sem):
    del xs_in
    _scatter_rows(dest_ref, h_ref, xs_ref, sem.at[0])


def _dispatch(dest3, h2x, xs):
    t, w = h2x.shape
    tr = ROW_TILE
    return pl.pallas_call(
        _dispatch_kernel,
        grid=(t // tr,),
        in_specs=[pl.BlockSpec((None, 1, tr), lambda i: (i, 0, 0), memory_space=pltpu.SMEM),
                  pl.BlockSpec((tr, w), lambda i: (i, 0)),
                  pl.BlockSpec(memory_space=pl.ANY)],
        out_specs=pl.BlockSpec(memory_space=pl.ANY),
        out_shape=jax.ShapeDtypeStruct(xs.shape, F32),
        scratch_shapes=[pltpu.SemaphoreType.DMA((1,))],
        input_output_aliases={2: 0},
        compiler_params=_cparams("arbitrary"),
        name="dispatch",
    )(dest3, h2x, xs)


def _moe_kernel(elo_ref, ehi_ref, nused_ref, xs_ref, wgu_lo, wd_lo, wgu_hi, wd_hi, ys_ref):
    del elo_ref, ehi_ref
    used = pl.program_id(0) < nused_ref[0]
    d = ys_ref.shape[1]

    @pl.when(used)
    def _():
        x = xs_ref[:, 0:d].astype(BF16)
        info = xs_ref[:, d:d + INFO_W]
        ff = wd_lo.shape[0]

        def expert(wgu, wd):
            gu = jnp.dot(x, wgu[...], preferred_element_type=F32)
            g = gu[:, :ff]
            u = gu[:, ff:]
            act = g * jax.nn.sigmoid(g) * u
            return jnp.dot(act.astype(BF16), wd[...], preferred_element_type=F32)

        ys_ref[...] = info[:, 2:3] * expert(wgu_lo, wd_lo) + info[:, 3:4] * expert(wgu_hi, wd_hi)

    @pl.when(jnp.logical_not(used))
    def _():
        ys_ref[...] = jnp.zeros(ys_ref.shape, F32)


def _moe(blk_elo, blk_ehi, n_used, xs, w_gu, w_d):
    n_rows, w = xs.shape
    d = w - INFO_W
    tb = MOE_BLOCK
    n_blk = n_rows // tb
    ff = w_d.shape[1]
    grid_spec = pltpu.PrefetchScalarGridSpec(
        num_scalar_prefetch=3,
        grid=(n_blk,),
        in_specs=[pl.BlockSpec((tb, w), lambda i, elo, ehi, nu: (jnp.minimum(i, nu[0] - 1), 0)),
                  pl.BlockSpec((None, d, 2 * ff), lambda i, elo, ehi, nu: (elo[i], 0, 0)),
                  pl.BlockSpec((None, ff, d), lambda i, elo, ehi, nu: (elo[i], 0, 0)),
                  pl.BlockSpec((None, d, 2 * ff), lambda i, elo, ehi, nu: (ehi[i], 0, 0)),
                  pl.BlockSpec((None, ff, d), lambda i, elo, ehi, nu: (ehi[i], 0, 0))],
        out_specs=pl.BlockSpec((tb, d), lambda i, elo, ehi, nu: (i, 0)),
    )
    return pl.pallas_call(
        _moe_kernel,
        grid_spec=grid_spec,
        out_shape=jax.ShapeDtypeStruct((n_rows, d), F32),
        compiler_params=_cparams("arbitrary"),
        name="moe",
    )(blk_elo, blk_ehi, n_used, xs, w_gu, w_d, w_gu, w_d)


def _combine_kernel(dest_ref, x1_ref, g2_ref, ys_ref, o_ref, buf_ref, sem):
    n = x1_ref.shape[0]
    _start_rows(n, lambda r: _row_copy(ys_ref, buf_ref, dest_ref[0, r], r, sem.at[0]))

    def wait(r, c):
        _row_copy(ys_ref, buf_ref, 0, 0, sem.at[0]).wait()
        return c

    lax.fori_loop(0, n, wait, 0, unroll=8)
    o_ref[...] = x1_ref[...] + g2_ref[...] * buf_ref[...]


def _combine(dest4, x1, g2, ys):
    b, s, d = x1.shape
    tr = ROW_TILE
    return pl.pallas_call(
        _combine_kernel,
        grid=(b, s // tr),
        in_specs=[pl.BlockSpec((None, None, 1, tr), lambda bb, i: (bb, i, 0, 0), memory_space=pltpu.SMEM),
                  pl.BlockSpec((None, tr, d), lambda bb, i: (bb, i, 0)),
                  pl.BlockSpec((None, 1, d), lambda bb, i: (bb, 0, 0)),
                  pl.BlockSpec(memory_space=pl.ANY)],
        out_specs=pl.BlockSpec((None, tr, d), lambda bb, i: (bb, i, 0)),
        out_shape=jax.ShapeDtypeStruct((b, s, d), F32),
        scratch_shapes=[pltpu.VMEM((tr, d), F32), pltpu.SemaphoreType.DMA((1,))],
        compiler_params=_cparams("arbitrary", "arbitrary"),
        name="combine",
    )(dest4, x1, g2, ys)


def _class_tables():
    elo, ehi = [], []
    for g in range(N_GROUPS):
        for a in range(EXPERTS_PER_GROUP):
            for b in range(a + 1, EXPERTS_PER_GROUP):
                elo.append(g * EXPERTS_PER_GROUP + a)
                ehi.append(g * EXPERTS_PER_GROUP + b)
    return np.asarray(elo, np.int32), np.asarray(ehi, np.int32)


def _rope_tables(s):
    inv_freq = ROPE_THETA ** (-jnp.arange(0, ROT_DIM, 2, dtype=F32) / ROT_DIM)
    ang = jnp.arange(s, dtype=F32)[:, None] * inv_freq[None, :]
    cos8, sin8 = jnp.cos(ang), jnp.sin(ang)
    ones = jnp.ones((s, HEAD_DIM - ROT_DIM), F32)
    cos64 = jnp.concatenate([cos8, cos8, ones], axis=1)
    sin64 = jnp.concatenate([-sin8, sin8, 0.0 * ones], axis=1)
    return jnp.tile(cos64, (1, 128 // HEAD_DIM)), jnp.tile(sin64, (1, 128 // HEAD_DIM))


def kernel(x_prompt, x_sample, c_prompt, c_sample, w_ada, b_ada, g_norm1, w_in, g_q, g_k, lambda_q1, lambda_k1, lambda_q2, lambda_k2, g_subln, w_dw, b_dw, g_conv_ln, b_conv_ln, w_out, g_norm2, w_router_group, b_router_group, w_router_expert, b_router_expert, w_gate_up, w_down):
    depth = w_ada.shape[0]
    d = x_prompt.shape[-1]
    qw = N_HEADS * 2 * HEAD_DIM
    aw = N_HEADS * V_DIM
    elo_tab, ehi_tab = _class_tables()
    seg = jnp.asarray(np.kron(np.eye(qw // HEAD_DIM), np.ones((HEAD_DIM, HEAD_DIM))), BF16)
    tri = jnp.asarray(np.tril(np.ones((ROUTE_TILE, ROUTE_TILE)), -1), BF16)
    groups = [x_prompt, x_sample]
    conds = [c_prompt, c_sample]
    nb = [c.shape[0] for c in conds]

    for l in range(depth):
        lam_init = 0.8 - 0.6 * math.exp(-0.3 * l)
        lam = (jnp.exp(jnp.sum(lambda_q1[l] * lambda_k1[l])) - jnp.exp(jnp.sum(lambda_q2[l] * lambda_k2[l]))
               + lam_init).reshape(1).astype(F32)
        mod = _ada(jnp.concatenate(conds, axis=0), w_ada[l], b_ada[l])
        w_in_b = w_in[l].astype(BF16)
        w_o_b = w_out[l][:aw].astype(BF16)
        w_c_b = w_out[l][aw:].astype(BF16)
        w_gu_b = w_gate_up[l].astype(BF16)
        w_d_b = w_down[l].astype(BF16)
        gq = jnp.tile(g_q[l], qw // HEAD_DIM).reshape(1, qw)
        gk = jnp.tile(g_k[l], qw // HEAD_DIM).reshape(1, qw)
        n_r = N_GROUPS + N_GROUPS * EXPERTS_PER_GROUP
        wr = jnp.zeros((d, 128), F32).at[:, :N_GROUPS].set(w_router_group[l]).at[:, N_GROUPS:n_r].set(
            w_router_expert[l]).astype(BF16)
        br = jnp.zeros((1, 128), F32).at[0, :N_GROUPS].set(b_router_group[l]).at[0, N_GROUPS:n_r].set(
            b_router_expert[l])

        x1s, h2xs, counts, g2s = [], [], [], []
        row0 = 0
        for x, n in zip(groups, nb):
            b, s, _ = x.shape
            m = mod[row0:row0 + n].reshape(n, 1, 6 * d)
            row0 += n
            sh1, sc1, g1, sh2, sc2, g2 = [m[:, :, i * d:(i + 1) * d] for i in range(6)]
            cos_t, sin_t = _rope_tables(s)
            q, k, vt, z = _proj(x, sh1, sc1, g_norm1[l].reshape(1, d), w_in_b, seg, gq, gk, cos_t, sin_t)
            o = _attention(lam, q, k, vt, g_subln[l].reshape(V_DIM, 1), 1.0 - lam_init)
            x1, h2x, cnt = _mix_route(z, w_dw[l], b_dw[l].reshape(1, -1), g_conv_ln[l].reshape(1, -1),
                                      b_conv_ln[l].reshape(1, -1), o, x, g1, w_o_b, w_c_b, sh2, sc2,
                                      g_norm2[l].reshape(1, d), wr, br, tri)
            x1s.append(x1)
            h2xs.append(h2x.reshape(b * s, d + INFO_W))
            counts.append(cnt[0, :N_CLASSES].astype(jnp.int32))
            g2s.append(g2)

        t_all = sum(h.shape[0] for h in h2xs)
        n_rows = t_all + N_CLASSES * MOE_BLOCK
        n_blk = n_rows // MOE_BLOCK
        tot = sum(counts)
        padded = ((tot + MOE_BLOCK - 1) // MOE_BLOCK) * MOE_BLOCK
        pad_end = jnp.cumsum(padded)
        pad_start = pad_end - padded
        dests, offs = [], jnp.zeros_like(tot)
        for h2x, cnt in zip(h2xs, counts):
            start_col = jnp.zeros((INFO_W, 1), F32).at[:N_CLASSES, 0].set((pad_start + offs).astype(F32))
            dests.append(_dest(h2x, start_col))
            offs = offs + cnt
        blk_cls = jnp.minimum(jnp.searchsorted(pad_end, jnp.arange(n_blk, dtype=jnp.int32) * MOE_BLOCK, side='right'),
                              N_CLASSES - 1)
        blk_elo = jnp.asarray(elo_tab)[blk_cls]
        blk_ehi = jnp.asarray(ehi_tab)[blk_cls]
        n_used = (pad_end[-1:] // MOE_BLOCK).astype(jnp.int32)

        xs = jnp.zeros((n_rows, d + INFO_W), F32)
        for h2x, dest in zip(h2xs, dests):
            xs = _dispatch(dest, h2x, xs)
        ys = _moe(blk_elo, blk_ehi, n_used, xs, w_gu_b, w_d_b)
        outs = []
        for x1, dest, g2 in zip(x1s, dests, g2s):
            b, s, _ = x1.shape
            outs.append(_combine(dest.reshape(b, s // ROW_TILE, 1, ROW_TILE), x1, g2, ys))
        groups = outs
    return tuple(groups)
```

```python
import functools
import math

import numpy as np
import jax
import jax.numpy as jnp
from jax import lax
from jax.experimental import pallas as pl
from jax.experimental.pallas import tpu as pltpu

F32 = jnp.float32
BF16 = jnp.bfloat16

N_HEADS = 4
HEAD_DIM = 64
V_DIM = 128
VT_ROWS = V_DIM + 16
ROT_DIM = 16
ROPE_THETA = 500000.0
CONV_KERNEL = 31
CONV_PAD = CONV_KERNEL // 2
N_GROUPS = 4
EXPERTS_PER_GROUP = 8
N_PAIRS = EXPERTS_PER_GROUP * (EXPERTS_PER_GROUP - 1) // 2
N_CLASSES = N_GROUPS * N_PAIRS
EPS = 1e-6
LOG2E = 1.4426950408889634
NEG_BIG = -1e30

VMEM_LIMIT_BYTES = 58 * 1024 * 1024
TOK_TILE = 512
INFO_W = 128
Q_TILE = 256
KEY_BLOCKS_PER_LOOP = 32
ROUTE_TILE = 512
CONV_CHUNK = 64
HALO = 16
CONV_SHIFT_CHUNK = (ROUTE_TILE + 2 * HALO) // 4
ROW_TILE = 2048
DMA_ROWS_PER_ITER = 8
MOE_BLOCK = 256


def _cparams(*sem):
    return pltpu.CompilerParams(dimension_semantics=sem, vmem_limit_bytes=VMEM_LIMIT_BYTES)


def _ada_kernel(c_ref, w_ref, b_ref, o_ref):
    c = c_ref[...]
    s = c * jax.nn.sigmoid(c)
    o_ref[...] = jnp.dot(s.astype(BF16), w_ref[...].astype(BF16), preferred_element_type=F32) + b_ref[...]


def _ada(c_all, w_ada, b_ada):
    nb, d = c_all.shape
    n = w_ada.shape[1]
    tn = n // 4
    return pl.pallas_call(
        _ada_kernel,
        grid=(n // tn,),
        in_specs=[pl.BlockSpec((nb, d), lambda j: (0, 0)),
                  pl.BlockSpec((d, tn), lambda j: (0, j)),
                  pl.BlockSpec((1, tn), lambda j: (0, j))],
        out_specs=pl.BlockSpec((nb, tn), lambda j: (0, j)),
        out_shape=jax.ShapeDtypeStruct((nb, n), F32),
        compiler_params=_cparams("arbitrary"),
        name="ada",
    )(c_all, w_ada, b_ada.reshape(1, n))


def _proj_kernel(x_ref, sh_ref, sc_ref, g1_ref, w_ref, seg_ref, gq_ref, gk_ref, cos_ref, sin_ref,
                 q_ref, k_ref, vt_ref, z_ref, *, q_scale):
    x = x_ref[...]
    ms = jnp.mean(x * x, axis=-1, keepdims=True)
    xn = x * lax.rsqrt(ms + EPS) * g1_ref[...]
    h = xn * (1.0 + sc_ref[...]) + sh_ref[...]
    proj = jnp.dot(h.astype(BF16), w_ref[...], preferred_element_type=F32)
    qw = N_HEADS * 2 * HEAD_DIM
    cos = cos_ref[...]
    sin = sin_ref[...]
    lane = lax.broadcasted_iota(jnp.int32, cos.shape, 1)
    first_half = jnp.bitwise_and(lane, HEAD_DIM - 1) < (ROT_DIM // 2)

    def norm_rot(t, g):
        ss = jnp.dot((t * t).astype(BF16), seg_ref[...], preferred_element_type=F32)
        tn = t * lax.rsqrt(ss * (1.0 / HEAD_DIM) + EPS) * g
        outs = []
        for j in range(qw // 128):
            tj = tn[:, j * 128:(j + 1) * 128]
            up = pltpu.roll(tj, 128 - ROT_DIM // 2, axis=1)
            dn = pltpu.roll(tj, ROT_DIM // 2, axis=1)
            outs.append(tj * cos + jnp.where(first_half, up, dn) * sin)
        return jnp.concatenate(outs, axis=1)

    q = norm_rot(proj[:, 0:qw], gq_ref[...])
    k = norm_rot(proj[:, qw:2 * qw], gk_ref[...])
    q_ref[...] = (q * q_scale).astype(BF16)
    k_ref[...] = k.astype(BF16)
    aw = N_HEADS * V_DIM
    v = proj[:, 2 * qw:2 * qw + aw]
    vt = v.T
    extra = (lax.broadcasted_iota(jnp.int32, (VT_ROWS - V_DIM, vt.shape[1]), 0) == 0).astype(F32)
    pieces = []
    for hh in range(N_HEADS):
        pieces += [vt[hh * V_DIM:(hh + 1) * V_DIM], extra]
    vt_ref[...] = jnp.concatenate(pieces, axis=0).astype(BF16)
    a = proj[:, 2 * qw + aw:2 * qw + 2 * aw]
    gate = proj[:, 2 * qw + 2 * aw:2 * qw + 3 * aw]
    z_ref[...] = a * jax.nn.sigmoid(gate)


def _proj(x, sh1, sc1, g1n, w_in, seg, gq, gk, cos_t, sin_t):
    b, s, d = x.shape
    tm = TOK_TILE
    qw = N_HEADS * 2 * HEAD_DIM
    aw = N_HEADS * V_DIM
    cw = w_in.shape[1] - 2 * qw - aw
    assert cw == 2 * aw and s % tm == 0
    row = lambda bb, i: (bb, 0, 0)
    const2 = lambda bb, i: (0, 0)
    tile = lambda bb, i: (bb, i, 0)
    kern = functools.partial(_proj_kernel, q_scale=HEAD_DIM ** -0.5 * LOG2E)
    return pl.pallas_call(
        kern,
        grid=(b, s // tm),
        in_specs=[pl.BlockSpec((None, tm, d), tile),
                  pl.BlockSpec((None, 1, d), row),
                  pl.BlockSpec((None, 1, d), row),
                  pl.BlockSpec((1, d), const2),
                  pl.BlockSpec(w_in.shape, const2),
                  pl.BlockSpec(seg.shape, const2),
                  pl.BlockSpec((1, qw), const2),
                  pl.BlockSpec((1, qw), const2),
                  pl.BlockSpec((tm, 128), lambda bb, i: (i, 0)),
                  pl.BlockSpec((tm, 128), lambda bb, i: (i, 0))],
        out_specs=[pl.BlockSpec((None, tm, qw), tile),
                   pl.BlockSpec((None, tm, qw), tile),
                   pl.BlockSpec((None, None, N_HEADS * VT_ROWS, tm), lambda bb, i: (bb, i, 0, 0)),
                   pl.BlockSpec((None, tm, aw), tile)],
        out_shape=[jax.ShapeDtypeStruct((b, s, qw), BF16),
                   jax.ShapeDtypeStruct((b, s, qw), BF16),
                   jax.ShapeDtypeStruct((b, s // tm, N_HEADS * VT_ROWS, tm), BF16),
                   jax.ShapeDtypeStruct((b, s, aw), F32)],
        compiler_params=_cparams("parallel", "parallel"),
        name="proj",
    )(x, sh1, sc1, g1n, w_in, seg, gq, gk, cos_t, sin_t)


def _attn_kernel(lam_ref, q_ref, k_ref, vt_ref, gs_ref, o_ref, s_ref, mb_ref, m_ref, acc_ref,
                 *, sk, mq, unroll, out_scale):
    nkb = vt_ref.shape[0]
    nq = q_ref.shape[0] // mq
    npairs = nq * nkb
    kb_bits = nkb.bit_length() - 1

    def init(qi, carry):
        m_ref[qi] = jnp.full(m_ref.shape[1:], NEG_BIG, F32)
        acc_ref[qi] = jnp.zeros(acc_ref.shape[1:], F32)
        return carry

    lax.fori_loop(0, nq, init, 0)

    def scores(t, slot):
        qi = lax.shift_right_logical(t, kb_bits)
        kb = jnp.bitwise_and(t, nkb - 1)
        q = q_ref[pl.ds(pl.multiple_of(qi * mq, mq), mq), :]
        lane = lax.broadcasted_iota(jnp.int32, q.shape, 1)
        zero = jnp.zeros_like(q)
        kblk = k_ref[pl.ds(pl.multiple_of(kb * sk, sk), sk), :]
        for c in range(2):
            qm = jnp.where((lane < HEAD_DIM) if c == 0 else (lane >= HEAD_DIM), q, zero)
            s = lax.dot_general(kblk, qm, (((1,), (1,)), ((), ())), preferred_element_type=F32)
            s_ref[slot, c] = s
            mb_ref[slot, c] = jnp.max(s, axis=0, keepdims=True)

    def accumulate(qi, kb, slot):
        vblk = vt_ref[kb]
        for c in range(2):
            m_old = m_ref[qi, c]
            m_new = jnp.maximum(m_old, mb_ref[slot, c])
            alpha = jnp.exp2(m_old - m_new)
            p = jnp.exp2(s_ref[slot, c] - m_new)
            acc_ref[qi, c] = alpha * acc_ref[qi, c] + jnp.dot(vblk, p.astype(BF16), preferred_element_type=F32)
            m_ref[qi, c] = m_new

    def group(t0, last):
        for i in range(unroll):
            if not (last and i == unroll - 1):
                scores(t0 + i + 1, (i + 1) % 2)
            t = t0 + i
            accumulate(lax.shift_right_logical(t, kb_bits), jnp.bitwise_and(t, nkb - 1), i % 2)

    scores(jnp.int32(0), 0)

    def body(j, carry):
        group(unroll * j, False)
        return carry

    lax.fori_loop(0, npairs // unroll - 1, body, 0)
    group(jnp.int32(npairs - unroll), True)
    lam = lam_ref[0]

    def finish(qi, carry):
        a0 = acc_ref[qi, 0]
        a1 = acc_ref[qi, 1]
        o = (a0[:V_DIM] * (1.0 / a0[V_DIM:V_DIM + 1])
             - lam * (a1[:V_DIM] * (1.0 / a1[V_DIM:V_DIM + 1])))
        ms = jnp.mean(o * o, axis=0, keepdims=True)
        on = o * lax.rsqrt(ms + EPS) * (gs_ref[...] * out_scale)
        o_ref[pl.ds(pl.multiple_of(qi * mq, mq), mq), :] = on.T.astype(BF16)
        return carry

    lax.fori_loop(0, nq, finish, 0)


def _attention(lam, q, k, vt, gsub_col, out_scale):
    b, s, qw = q.shape
    nkb, _, sk = vt.shape[1:]
    aw = N_HEADS * V_DIM
    mq = Q_TILE
    nq = s // mq
    unroll = min(nq * nkb, KEY_BLOCKS_PER_LOOP)
    assert unroll % 2 == 0 and (nq * nkb) % unroll == 0 and nkb & (nkb - 1) == 0
    kern = functools.partial(_attn_kernel, sk=sk, mq=mq, unroll=unroll, out_scale=out_scale)
    head = lambda bb, h: (bb, 0, h)
    return pl.pallas_call(
        kern,
        grid=(b, N_HEADS),
        in_specs=[pl.BlockSpec(memory_space=pltpu.SMEM),
                  pl.BlockSpec((None, s, 128), head),
                  pl.BlockSpec((None, s, 128), head),
                  pl.BlockSpec((None, nkb, VT_ROWS, sk), lambda bb, h: (bb, 0, h, 0)),
                  pl.BlockSpec((V_DIM, 1), lambda bb, h: (0, 0))],
        out_specs=pl.BlockSpec((None, s, V_DIM), head),
        out_shape=jax.ShapeDtypeStruct((b, s, aw), BF16),
        scratch_shapes=[pltpu.VMEM((2, 2, sk, mq), F32), pltpu.VMEM((2, 2, 1, mq), F32),
                        pltpu.VMEM((nq, 2, 1, mq), F32), pltpu.VMEM((nq, 2, VT_ROWS, mq), F32)],
        compiler_params=_cparams("parallel", "parallel"),
        name="attn",
    )(lam, q, k, vt, gsub_col)


def _conv_tile(t, prev_ref, cur_ref, next_ref, w_ref, b_ref, g_ref, beta_ref, o_ref, ext_ref, sh_ref):
    tc, c = cur_ref.shape
    n_ext = tc + 2 * HALO
    keep_prev = (t > 0).astype(F32)
    keep_next = (t < pl.num_programs(1) - 1).astype(F32)
    ext_ref[0:HALO, :] = prev_ref[...] * keep_prev
    ext_ref[HALO:HALO + tc, :] = cur_ref[...]
    ext_ref[HALO + tc:n_ext, :] = next_ref[...] * keep_next
    ext_ref[n_ext:n_ext + 8, :] = jnp.zeros((8, c), F32)
    for o in range(1, 8):
        for e0 in range(0, n_ext, CONV_SHIFT_CHUNK):
            sh_ref[o - 1, e0:e0 + CONV_SHIFT_CHUNK, :] = ext_ref[e0 + o:e0 + o + CONV_SHIFT_CHUNK, :]
    w = w_ref[...]
    base = HALO - CONV_PAD
    for r0 in range(0, tc, CONV_CHUNK):
        acc = jnp.broadcast_to(b_ref[...], (CONV_CHUNK, c))
        for j in range(CONV_KERNEL):
            o = (base + j) % 8
            a0 = r0 + base + j - o
            src = ext_ref if o == 0 else sh_ref.at[o - 1]
            acc = acc + w[j:j + 1, :] * src[a0:a0 + CONV_CHUNK, :]
        mu = jnp.mean(acc, axis=-1, keepdims=True)
        cen = acc - mu
        var = jnp.mean(cen * cen, axis=-1, keepdims=True)
        y = cen * lax.rsqrt(var + EPS) * g_ref[...] + beta_ref[...]
        o_ref[r0:r0 + CONV_CHUNK, :] = (y * jax.nn.sigmoid(y)).astype(BF16)


def _mix_route_kernel(zp_ref, z_ref, zn_ref, wdw_ref, bdw_ref, gln_ref, bln_ref,
                      o_ref, x_ref, g1_ref, wo_ref, wc_ref, sh_ref, sc_ref, g2n_ref, wr_ref, br_ref, tri_ref,
                      x1_ref, h2x_ref, cnt_ref, ext_ref, shift_ref, cv_ref, carry_ref):
    first = jnp.logical_and(pl.program_id(0) == 0, pl.program_id(1) == 0)

    @pl.when(first)
    def _():
        carry_ref[...] = jnp.zeros(carry_ref.shape, F32)

    _conv_tile(pl.program_id(1), zp_ref, z_ref, zn_ref, wdw_ref, bdw_ref, gln_ref, bln_ref, cv_ref, ext_ref,
               shift_ref)
    mix = jnp.dot(o_ref[...], wo_ref[...], preferred_element_type=F32)
    mix = mix + jnp.dot(cv_ref[...], wc_ref[...], preferred_element_type=F32)
    x = x_ref[...] + g1_ref[...] * mix
    x1_ref[...] = x
    d = x.shape[1]
    ms = jnp.mean(x * x, axis=-1, keepdims=True)
    h2 = x * lax.rsqrt(ms + EPS) * g2n_ref[...] * (1.0 + sc_ref[...]) + sh_ref[...]
    h2x_ref[:, 0:d] = h2
    lg = jnp.dot(h2.astype(BF16), wr_ref[...], preferred_element_type=F32) + br_ref[...]
    lane = lax.broadcasted_iota(jnp.int32, lg.shape, 1).astype(F32)
    far = jnp.float32(1e9)
    in_g = lane < N_GROUPS
    lgm = jnp.where(in_g, lg, NEG_BIG)
    gmax = jnp.max(lgm, axis=1, keepdims=True)
    gsel = jnp.min(jnp.where(lgm == gmax, lane, far), axis=1, keepdims=True)
    p_group = 1.0 / jnp.sum(jnp.where(in_g, jnp.exp(lg - gmax), 0.0), axis=1, keepdims=True)
    lo_lane = N_GROUPS + gsel * EXPERTS_PER_GROUP
    in_e = jnp.logical_and(lane >= lo_lane, lane < lo_lane + EXPERTS_PER_GROUP)
    le = jnp.where(in_e, lg, NEG_BIG)
    v1 = jnp.max(le, axis=1, keepdims=True)
    l1 = jnp.min(jnp.where(le == v1, lane, far), axis=1, keepdims=True)
    le2 = jnp.where(lane == l1, NEG_BIG, le)
    v2 = jnp.max(le2, axis=1, keepdims=True)
    l2 = jnp.min(jnp.where(le2 == v2, lane, far), axis=1, keepdims=True)
    r = jnp.exp(v2 - v1)
    w1 = p_group / (1.0 + r)
    w2 = w1 * r
    i1 = l1 - lo_lane
    i2 = l2 - lo_lane
    a = jnp.minimum(i1, i2)
    b = jnp.maximum(i1, i2)
    w_lo = jnp.where(i1 < i2, w1, w2)
    w_hi = jnp.where(i1 < i2, w2, w1)
    cls = gsel * N_PAIRS + a * (2 * EXPERTS_PER_GROUP - 1 - a) * 0.5 + (b - a - 1.0)
    onehot = (lane == cls).astype(F32)
    excl = jnp.dot(tri_ref[...], onehot.astype(BF16), preferred_element_type=F32)
    rank = jnp.sum(onehot * (excl + carry_ref[...]), axis=1, keepdims=True)
    carry_ref[...] = carry_ref[...] + jnp.sum(onehot, axis=0, keepdims=True)
    info = jnp.where(lane == 0, cls, jnp.where(lane == 1, rank, jnp.where(lane == 2, w_lo,
                                                                           jnp.where(lane == 3, w_hi, 0.0))))
    h2x_ref[:, d:d + INFO_W] = info
    cnt_ref[...] = jnp.broadcast_to(carry_ref[...], cnt_ref.shape)


def _mix_route(z, w_dw, b_dw, g_ln, b_ln, o, x, g1, w_o, w_c, sh2, sc2, g2n, wr, br, tri):
    b, s, d = x.shape
    tr = ROUTE_TILE
    aw = o.shape[-1]
    c = z.shape[-1]
    nh = tr // HALO
    last = s // HALO - 1
    tile = lambda bb, i: (bb, i, 0)
    row = lambda bb, i: (bb, 0, 0)
    const2 = lambda bb, i: (0, 0)
    return pl.pallas_call(
        _mix_route_kernel,
        grid=(b, s // tr),
        in_specs=[pl.BlockSpec((None, HALO, c), lambda bb, i: (bb, jnp.maximum(i * nh - 1, 0), 0)),
                  pl.BlockSpec((None, tr, c), tile),
                  pl.BlockSpec((None, HALO, c), lambda bb, i: (bb, jnp.minimum((i + 1) * nh, last), 0)),
                  pl.BlockSpec((CONV_KERNEL, c), const2),
                  pl.BlockSpec((1, c), const2), pl.BlockSpec((1, c), const2), pl.BlockSpec((1, c), const2),
                  pl.BlockSpec((None, tr, aw), tile),
                  pl.BlockSpec((None, tr, d), tile), pl.BlockSpec((None, 1, d), row),
                  pl.BlockSpec(w_o.shape, const2), pl.BlockSpec(w_c.shape, const2),
                  pl.BlockSpec((None, 1, d), row), pl.BlockSpec((None, 1, d), row),
                  pl.BlockSpec((1, d), const2), pl.BlockSpec(wr.shape, const2), pl.BlockSpec((1, 128), const2),
                  pl.BlockSpec((tr, tr), const2)],
        out_specs=[pl.BlockSpec((None, tr, d), tile), pl.BlockSpec((None, tr, d + INFO_W), tile),
                   pl.BlockSpec((8, 128), const2)],
        out_shape=[jax.ShapeDtypeStruct((b, s, d), F32), jax.ShapeDtypeStruct((b, s, d + INFO_W), F32),
                   jax.ShapeDtypeStruct((8, 128), F32)],
        scratch_shapes=[pltpu.VMEM((tr + 2 * HALO + 8, c), F32), pltpu.VMEM((7, tr + 2 * HALO, c), F32),
                        pltpu.VMEM((tr, c), BF16), pltpu.VMEM((1, 128), F32)],
        compiler_params=_cparams("arbitrary", "arbitrary"),
        name="mix_route",
    )(z, z, z, w_dw, b_dw, g_ln, b_ln, o, x, g1, w_o, w_c, sh2, sc2, g2n, wr, br, tri)


def _dest_kernel(info_ref, start_ref, dest_ref):
    info_t = info_ref[...].T
    cls = info_t[0:1, :]
    rank = info_t[1:2, :]
    cid = lax.broadcasted_iota(jnp.int32, info_t.shape, 0).astype(F32)
    base = jnp.sum(jnp.where(cid == cls, start_ref[...], 0.0), axis=0, keepdims=True)
    dest_ref[...] = (base + rank).astype(jnp.int32)


def _dest(h2x, start_col):
    t, w = h2x.shape
    tr = ROW_TILE
    return pl.pallas_call(
        _dest_kernel,
        grid=(t // tr,),
        in_specs=[pl.BlockSpec((tr, INFO_W), lambda i: (i, (w - INFO_W) // INFO_W)),
                  pl.BlockSpec((INFO_W, 1), lambda i: (0, 0))],
        out_specs=pl.BlockSpec((None, 1, tr), lambda i: (i, 0, 0)),
        out_shape=jax.ShapeDtypeStruct((t // tr, 1, tr), jnp.int32),
        compiler_params=_cparams("parallel"),
        name="dest",
    )(h2x, start_col)


def _row_copy(src_ref, dst_ref, src_row, dst_row, sem):
    return pltpu.make_async_copy(src_ref.at[pl.ds(src_row, 1), :], dst_ref.at[pl.ds(dst_row, 1), :], sem)


def _start_rows(n, copy_of_row):
    def start(g, c):
        for u in range(DMA_ROWS_PER_ITER):
            copy_of_row(g * DMA_ROWS_PER_ITER + u).start(priority=u % 2)
        return c

    lax.fori_loop(0, n // DMA_ROWS_PER_ITER, start, 0)


def _scatter_rows(dest_ref, h_ref, xs_ref, sem):
    n = h_ref.shape[0]
    _start_rows(n, lambda r: _row_copy(h_ref, xs_ref, r, dest_ref[0, r], sem))

    def wait(r, c):
        _row_copy(h_ref, xs_ref, 0, 0, sem).wait()
        return c

    lax.fori_loop(0, n, wait, 0, unroll=8)


def _dispatch_kernel(dest_ref, h_ref, xs_in, xs_ref, sem):
    del xs_in
    _scatter_rows(dest_ref, h_ref, xs_ref, sem.at[0])


def _dispatch(dest3, h2x, xs):
    t, w = h2x.shape
    tr = ROW_TILE
    return pl.pallas_call(
        _dispatch_kernel,
        grid=(t // tr,),
        in_specs=[pl.BlockSpec((None, 1, tr), lambda i: (i, 0, 0), memory_space=pltpu.SMEM),
                  pl.BlockSpec((tr, w), lambda i: (i, 0)),
                  pl.BlockSpec(memory_space=pl.ANY)],
        out_specs=pl.BlockSpec(memory_space=pl.ANY),
        out_shape=jax.ShapeDtypeStruct(xs.shape, F32),
        scratch_shapes=[pltpu.SemaphoreType.DMA((1,))],
        input_output_aliases={2: 0},
        compiler_params=_cparams("arbitrary"),
        name="dispatch",
    )(dest3, h2x, xs)


def _moe_kernel(elo_ref, ehi_ref, nused_ref, xs_ref, wgu_lo, wd_lo, wgu_hi, wd_hi, ys_ref):
    del elo_ref, ehi_ref
    used = pl.program_id(0) < nused_ref[0]
    d = ys_ref.shape[1]

    @pl.when(used)
    def _():
        x = xs_ref[:, 0:d].astype(BF16)
        info = xs_ref[:, d:d + INFO_W]
        ff = wd_lo.shape[0]

        def expert(wgu, wd):
            gu = jnp.dot(x, wgu[...], preferred_element_type=F32)
            g = gu[:, :ff]
            u = gu[:, ff:]
            act = g * jax.nn.sigmoid(g) * u
            return jnp.dot(act.astype(BF16), wd[...], preferred_element_type=F32)

        ys_ref[...] = info[:, 2:3] * expert(wgu_lo, wd_lo) + info[:, 3:4] * expert(wgu_hi, wd_hi)

    @pl.when(jnp.logical_not(used))
    def _():
        ys_ref[...] = jnp.zeros(ys_ref.shape, F32)


def _moe(blk_elo, blk_ehi, n_used, xs, w_gu, w_d):
    n_rows, w = xs.shape
    d = w - INFO_W
    tb = MOE_BLOCK
    n_blk = n_rows // tb
    ff = w_d.shape[1]
    grid_spec = pltpu.PrefetchScalarGridSpec(
        num_scalar_prefetch=3,
        grid=(n_blk,),
        in_specs=[pl.BlockSpec((tb, w), lambda i, elo, ehi, nu: (jnp.minimum(i, nu[0] - 1), 0)),
                  pl.BlockSpec((None, d, 2 * ff), lambda i, elo, ehi, nu: (elo[i], 0, 0)),
                  pl.BlockSpec((None, ff, d), lambda i, elo, ehi, nu: (elo[i], 0, 0)),
                  pl.BlockSpec((None, d, 2 * ff), lambda i, elo, ehi, nu: (ehi[i], 0, 0)),
                  pl.BlockSpec((None, ff, d), lambda i, elo, ehi, nu: (ehi[i], 0, 0))],
        out_specs=pl.BlockSpec((tb, d), lambda i, elo, ehi, nu: (i, 0)),
    )
    return pl.pallas_call(
        _moe_kernel,
        grid_spec=grid_spec,
        out_shape=jax.ShapeDtypeStruct((n_rows, d), F32),
        compiler_params=_cparams("arbitrary"),
        name="moe",
    )(blk_elo, blk_ehi, n_used, xs, w_gu, w_d, w_gu, w_d)


def _combine_kernel(dest_ref, x1_ref, g2_ref, ys_ref, o_ref, buf_ref, sem):
    n = x1_ref.shape[0]
    _start_rows(n, lambda r: _row_copy(ys_ref, buf_ref, dest_ref[0, r], r, sem.at[0]))

    def wait(r, c):
        _row_copy(ys_ref, buf_ref, 0, 0, sem.at[0]).wait()
        return c

    lax.fori_loop(0, n, wait, 0, unroll=8)
    o_ref[...] = x1_ref[...] + g2_ref[...] * buf_ref[...]


def _combine(dest4, x1, g2, ys):
    b, s, d = x1.shape
    tr = ROW_TILE
    return pl.pallas_call(
        _combine_kernel,
        grid=(b, s // tr),
        in_specs=[pl.BlockSpec((None, None, 1, tr), lambda bb, i: (bb, i, 0, 0), memory_space=pltpu.SMEM),
                  pl.BlockSpec((None, tr, d), lambda bb, i: (bb, i, 0)),
                  pl.BlockSpec((None, 1, d), lambda bb, i: (bb, 0, 0)),
                  pl.BlockSpec(memory_space=pl.ANY)],
        out_specs=pl.BlockSpec((None, tr, d), lambda bb, i: (bb, i, 0)),
        out_shape=jax.ShapeDtypeStruct((b, s, d), F32),
        scratch_shapes=[pltpu.VMEM((tr, d), F32), pltpu.SemaphoreType.DMA((1,))],
        compiler_params=_cparams("arbitrary", "arbitrary"),
        name="combine",
    )(dest4, x1, g2, ys)


def _class_tables():
    elo, ehi = [], []
    for g in range(N_GROUPS):
        for a in range(EXPERTS_PER_GROUP):
            for b in range(a + 1, EXPERTS_PER_GROUP):
                elo.append(g * EXPERTS_PER_GROUP + a)
                ehi.append(g * EXPERTS_PER_GROUP + b)
    return np.asarray(elo, np.int32), np.asarray(ehi, np.int32)


def _rope_tables(s):
    inv_freq = ROPE_THETA ** (-jnp.arange(0, ROT_DIM, 2, dtype=F32) / ROT_DIM)
    ang = jnp.arange(s, dtype=F32)[:, None] * inv_freq[None, :]
    cos8, sin8 = jnp.cos(ang), jnp.sin(ang)
    ones = jnp.ones((s, HEAD_DIM - ROT_DIM), F32)
    cos64 = jnp.concatenate([cos8, cos8, ones], axis=1)
    sin64 = jnp.concatenate([-sin8, sin8, 0.0 * ones], axis=1)
    return jnp.tile(cos64, (1, 128 // HEAD_DIM)), jnp.tile(sin64, (1, 128 // HEAD_DIM))


def kernel(x_prompt, x_sample, c_prompt, c_sample, w_ada, b_ada, g_norm1, w_in, g_q, g_k, lambda_q1, lambda_k1, lambda_q2, lambda_k2, g_subln, w_dw, b_dw, g_conv_ln, b_conv_ln, w_out, g_norm2, w_router_group, b_router_group, w_router_expert, b_router_expert, w_gate_up, w_down):
    depth = w_ada.shape[0]
    d = x_prompt.shape[-1]
    qw = N_HEADS * 2 * HEAD_DIM
    aw = N_HEADS * V_DIM
    elo_tab, ehi_tab = _class_tables()
    seg = jnp.asarray(np.kron(np.eye(qw // HEAD_DIM), np.ones((HEAD_DIM, HEAD_DIM))), BF16)
    tri = jnp.asarray(np.tril(np.ones((ROUTE_TILE, ROUTE_TILE)), -1), BF16)
    groups = [x_prompt, x_sample]
    conds = [c_prompt, c_sample]
    nb = [c.shape[0] for c in conds]

    for l in range(depth):
        lam_init = 0.8 - 0.6 * math.exp(-0.3 * l)
        lam = (jnp.exp(jnp.sum(lambda_q1[l] * lambda_k1[l])) - jnp.exp(jnp.sum(lambda_q2[l] * lambda_k2[l]))
               + lam_init).reshape(1).astype(F32)
        mod = _ada(jnp.concatenate(conds, axis=0), w_ada[l], b_ada[l])
        w_in_b = w_in[l].astype(BF16)
        w_o_b = w_out[l][:aw].astype(BF16)
        w_c_b = w_out[l][aw:].astype(BF16)
        w_gu_b = w_gate_up[l].astype(BF16)
        w_d_b = w_down[l].astype(BF16)
        gq = jnp.tile(g_q[l], qw // HEAD_DIM).reshape(1, qw)
        gk = jnp.tile(g_k[l], qw // HEAD_DIM).reshape(1, qw)
        n_r = N_GROUPS + N_GROUPS * EXPERTS_PER_GROUP
        wr = jnp.zeros((d, 128), F32).at[:, :N_GROUPS].set(w_router_group[l]).at[:, N_GROUPS:n_r].set(
            w_router_expert[l]).astype(BF16)
        br = jnp.zeros((1, 128), F32).at[0, :N_GROUPS].set(b_router_group[l]).at[0, N_GROUPS:n_r].set(
            b_router_expert[l])

        x1s, h2xs, counts, g2s = [], [], [], []
        row0 = 0
        for x, n in zip(groups, nb):
            b, s, _ = x.shape
            m = mod[row0:row0 + n].reshape(n, 1, 6 * d)
            row0 += n
            sh1, sc1, g1, sh2, sc2, g2 = [m[:, :, i * d:(i + 1) * d] for i in range(6)]
            cos_t, sin_t = _rope_tables(s)
            q, k, vt, z = _proj(x, sh1, sc1, g_norm1[l].reshape(1, d), w_in_b, seg, gq, gk, cos_t, sin_t)
            o = _attention(lam, q, k, vt, g_subln[l].reshape(V_DIM, 1), 1.0 - lam_init)
            x1, h2x, cnt = _mix_route(z, w_dw[l], b_dw[l].reshape(1, -1), g_conv_ln[l].reshape(1, -1),
                                      b_conv_ln[l].reshape(1, -1), o, x, g1, w_o_b, w_c_b, sh2, sc2,
                                      g_norm2[l].reshape(1, d), wr, br, tri)
            x1s.append(x1)
            h2xs.append(h2x.reshape(b * s, d + INFO_W))
            counts.append(cnt[0, :N_CLASSES].astype(jnp.int32))
            g2s.append(g2)

        t_all = sum(h.shape[0] for h in h2xs)
        n_rows = t_all + N_CLASSES * MOE_BLOCK
        n_blk = n_rows // MOE_BLOCK
        tot = sum(counts)
        padded = ((tot + MOE_BLOCK - 1) // MOE_BLOCK) * MOE_BLOCK
        pad_end = jnp.cumsum(padded)
        pad_start = pad_end - padded
        dests, offs = [], jnp.zeros_like(tot)
        for h2x, cnt in zip(h2xs, counts):
            start_col = jnp.zeros((INFO_W, 1), F32).at[:N_CLASSES, 0].set((pad_start + offs).astype(F32))
            dests.append(_dest(h2x, start_col))
            offs = offs + cnt
        blk_cls = jnp.minimum(jnp.searchsorted(pad_end, jnp.arange(n_blk, dtype=jnp.int32) * MOE_BLOCK, side='right'),
                              N_CLASSES - 1)
        blk_elo = jnp.asarray(elo_tab)[blk_cls]
        blk_ehi = jnp.asarray(ehi_tab)[blk_cls]
        n_used = (pad_end[-1:] // MOE_BLOCK).astype(jnp.int32)

        xs = jnp.zeros((n_rows, d + INFO_W), F32)
        for h2x, dest in zip(h2xs, dests):
            xs = _dispatch(dest, h2x, xs)
        ys = _moe(blk_elo, blk_ehi, n_used, xs, w_gu_b, w_d_b)
        outs = []
        for x1, dest, g2 in zip(x1s, dests, g2s):
            b, s, _ = x1.shape
            outs.append(_combine(dest.reshape(b, s // ROW_TILE, 1, ROW_TILE), x1, g2, ys))
        groups = outs
    return tuple(groups)
```

```python
import functools
import math

import numpy as np
import jax
import jax.numpy as jnp
from jax import lax
from jax.experimental import pallas as pl
from jax.experimental.pallas import tpu as pltpu

F32 = jnp.float32
BF16 = jnp.bfloat16

N_HEADS = 4
HEAD_DIM = 64
V_DIM = 128
VT_ROWS = V_DIM + 16
ROT_DIM = 16
ROPE_THETA = 500000.0
CONV_KERNEL = 31
CONV_PAD = CONV_KERNEL // 2
N_GROUPS = 4
EXPERTS_PER_GROUP = 8
N_PAIRS = EXPERTS_PER_GROUP * (EXPERTS_PER_GROUP - 1) // 2
N_CLASSES = N_GROUPS * N_PAIRS
EPS = 1e-6
LOG2E = 1.4426950408889634
NEG_BIG = -1e30

VMEM_LIMIT_BYTES = 58 * 1024 * 1024
TOK_TILE = 512
INFO_W = 128
Q_TILE = 256
KEY_BLOCKS_PER_LOOP = 32
ROUTE_TILE = 512
CONV_CHUNK = 64
HALO = 16
CONV_SHIFT_CHUNK = (ROUTE_TILE + 2 * HALO) // 4
ROW_TILE = 1024
DMA_ROWS_PER_ITER = 8
MOE_BLOCK = 256


def _cparams(*sem):
    return pltpu.CompilerParams(dimension_semantics=sem, vmem_limit_bytes=VMEM_LIMIT_BYTES)


def _ada_kernel(c_ref, w_ref, b_ref, o_ref):
    c = c_ref[...]
    s = c * jax.nn.sigmoid(c)
    o_ref[...] = jnp.dot(s.astype(BF16), w_ref[...].astype(BF16), preferred_element_type=F32) + b_ref[...]


def _ada(c_all, w_ada, b_ada):
    nb, d = c_all.shape
    n = w_ada.shape[1]
    tn = n // 4
    return pl.pallas_call(
        _ada_kernel,
        grid=(n // tn,),
        in_specs=[pl.BlockSpec((nb, d), lambda j: (0, 0)),
                  pl.BlockSpec((d, tn), lambda j: (0, j)),
                  pl.BlockSpec((1, tn), lambda j: (0, j))],
        out_specs=pl.BlockSpec((nb, tn), lambda j: (0, j)),
        out_shape=jax.ShapeDtypeStruct((nb, n), F32),
        compiler_params=_cparams("arbitrary"),
        name="ada",
    )(c_all, w_ada, b_ada.reshape(1, n))


def _proj_kernel(x_ref, sh_ref, sc_ref, g1_ref, w_ref, seg_ref, gq_ref, gk_ref, cos_ref, sin_ref,
                 q_ref, k_ref, vt_ref, z_ref, *, q_scale):
    x = x_ref[...]
    ms = jnp.mean(x * x, axis=-1, keepdims=True)
    xn = x * lax.rsqrt(ms + EPS) * g1_ref[...]
    h = xn * (1.0 + sc_ref[...]) + sh_ref[...]
    proj = jnp.dot(h.astype(BF16), w_ref[...], preferred_element_type=F32)
    qw = N_HEADS * 2 * HEAD_DIM
    cos = cos_ref[...]
    sin = sin_ref[...]
    lane = lax.broadcasted_iota(jnp.int32, cos.shape, 1)
    first_half = jnp.bitwise_and(lane, HEAD_DIM - 1) < (ROT_DIM // 2)

    def norm_rot(t, g):
        ss = jnp.dot((t * t).astype(BF16), seg_ref[...], preferred_element_type=F32)
        tn = t * lax.rsqrt(ss * (1.0 / HEAD_DIM) + EPS) * g
        outs = []
        for j in range(qw // 128):
            tj = tn[:, j * 128:(j + 1) * 128]
            up = pltpu.roll(tj, 128 - ROT_DIM // 2, axis=1)
            dn = pltpu.roll(tj, ROT_DIM // 2, axis=1)
            outs.append(tj * cos + jnp.where(first_half, up, dn) * sin)
        return jnp.concatenate(outs, axis=1)

    q = norm_rot(proj[:, 0:qw], gq_ref[...])
    k = norm_rot(proj[:, qw:2 * qw], gk_ref[...])
    q_ref[...] = (q * q_scale).astype(BF16)
    k_ref[...] = k.astype(BF16)
    aw = N_HEADS * V_DIM
    v = proj[:, 2 * qw:2 * qw + aw]
    vt = v.T
    extra = (lax.broadcasted_iota(jnp.int32, (VT_ROWS - V_DIM, vt.shape[1]), 0) == 0).astype(F32)
    pieces = []
    for hh in range(N_HEADS):
        pieces += [vt[hh * V_DIM:(hh + 1) * V_DIM], extra]
    vt_ref[...] = jnp.concatenate(pieces, axis=0).astype(BF16)
    a = proj[:, 2 * qw + aw:2 * qw + 2 * aw]
    gate = proj[:, 2 * qw + 2 * aw:2 * qw + 3 * aw]
    z_ref[...] = a * jax.nn.sigmoid(gate)


def _proj(x, sh1, sc1, g1n, w_in, seg, gq, gk, cos_t, sin_t):
    b, s, d = x.shape
    tm = TOK_TILE
    qw = N_HEADS * 2 * HEAD_DIM
    aw = N_HEADS * V_DIM
    cw = w_in.shape[1] - 2 * qw - aw
    assert cw == 2 * aw and s % tm == 0
    row = lambda bb, i: (bb, 0, 0)
    const2 = lambda bb, i: (0, 0)
    tile = lambda bb, i: (bb, i, 0)
    kern = functools.partial(_proj_kernel, q_scale=HEAD_DIM ** -0.5 * LOG2E)
    return pl.pallas_call(
        kern,
        grid=(b, s // tm),
        in_specs=[pl.BlockSpec((None, tm, d), tile),
                  pl.BlockSpec((None, 1, d), row),
                  pl.BlockSpec((None, 1, d), row),
                  pl.BlockSpec((1, d), const2),
                  pl.BlockSpec(w_in.shape, const2),
                  pl.BlockSpec(seg.shape, const2),
                  pl.BlockSpec((1, qw), const2),
                  pl.BlockSpec((1, qw), const2),
                  pl.BlockSpec((tm, 128), lambda bb, i: (i, 0)),
                  pl.BlockSpec((tm, 128), lambda bb, i: (i, 0))],
        out_specs=[pl.BlockSpec((None, tm, qw), tile),
                   pl.BlockSpec((None, tm, qw), tile),
                   pl.BlockSpec((None, None, N_HEADS * VT_ROWS, tm), lambda bb, i: (bb, i, 0, 0)),
                   pl.BlockSpec((None, tm, aw), tile)],
        out_shape=[jax.ShapeDtypeStruct((b, s, qw), BF16),
                   jax.ShapeDtypeStruct((b, s, qw), BF16),
                   jax.ShapeDtypeStruct((b, s // tm, N_HEADS * VT_ROWS, tm), BF16),
                   jax.ShapeDtypeStruct((b, s, aw), F32)],
        compiler_params=_cparams("parallel", "parallel"),
        name="proj",
    )(x, sh1, sc1, g1n, w_in, seg, gq, gk, cos_t, sin_t)


def _attn_kernel(lam_ref, q_ref, k_ref, vt_ref, gs_ref, o_ref, s_ref, mb_ref, m_ref, acc_ref,
                 *, sk, mq, unroll, out_scale):
    nkb = vt_ref.shape[0]
    nq = q_ref.shape[0] // mq
    npairs = nq * nkb
    kb_bits = nkb.bit_length() - 1

    def init(qi, carry):
        m_ref[qi] = jnp.full(m_ref.shape[1:], NEG_BIG, F32)
        acc_ref[qi] = jnp.zeros(acc_ref.shape[1:], F32)
        return carry

    lax.fori_loop(0, nq, init, 0)

    def scores(t, slot):
        qi = lax.shift_right_logical(t, kb_bits)
        kb = jnp.bitwise_and(t, nkb - 1)
        q = q_ref[pl.ds(pl.multiple_of(qi * mq, mq), mq), :]
        lane = lax.broadcasted_iota(jnp.int32, q.shape, 1)
        zero = jnp.zeros_like(q)
        kblk = k_ref[pl.ds(pl.multiple_of(kb * sk, sk), sk), :]
        for c in range(2):
            qm = jnp.where((lane < HEAD_DIM) if c == 0 else (lane >= HEAD_DIM), q, zero)
            s = lax.dot_general(kblk, qm, (((1,), (1,)), ((), ())), preferred_element_type=F32)
            s_ref[slot, c] = s
            mb_ref[slot, c] = jnp.max(s, axis=0, keepdims=True)

    def accumulate(qi, kb, slot):
        vblk = vt_ref[kb]
        for c in range(2):
            m_old = m_ref[qi, c]
            m_new = jnp.maximum(m_old, mb_ref[slot, c])
            alpha = jnp.exp2(m_old - m_new)
            p = jnp.exp2(s_ref[slot, c] - m_new)
            acc_ref[qi, c] = alpha * acc_ref[qi, c] + jnp.dot(vblk, p.astype(BF16), preferred_element_type=F32)
            m_ref[qi, c] = m_new

    def group(t0, last):
        for i in range(unroll):
            if not (last and i == unroll - 1):
                scores(t0 + i + 1, (i + 1) % 2)
            t = t0 + i
            accumulate(lax.shift_right_logical(t, kb_bits), jnp.bitwise_and(t, nkb - 1), i % 2)

    scores(jnp.int32(0), 0)

    def body(j, carry):
        group(unroll * j, False)
        return carry

    lax.fori_loop(0, npairs // unroll - 1, body, 0)
    group(jnp.int32(npairs - unroll), True)
    lam = lam_ref[0]

    def finish(qi, carry):
        a0 = acc_ref[qi, 0]
        a1 = acc_ref[qi, 1]
        o = (a0[:V_DIM] * (1.0 / a0[V_DIM:V_DIM + 1])
             - lam * (a1[:V_DIM] * (1.0 / a1[V_DIM:V_DIM + 1])))
        ms = jnp.mean(o * o, axis=0, keepdims=True)
        on = o * lax.rsqrt(ms + EPS) * (gs_ref[...] * out_scale)
        o_ref[pl.ds(pl.multiple_of(qi * mq, mq), mq), :] = on.T.astype(BF16)
        return carry

    lax.fori_loop(0, nq, finish, 0)


def _attention(lam, q, k, vt, gsub_col, out_scale):
    b, s, qw = q.shape
    nkb, _, sk = vt.shape[1:]
    aw = N_HEADS * V_DIM
    mq = Q_TILE
    nq = s // mq
    unroll = min(nq * nkb, KEY_BLOCKS_PER_LOOP)
    assert unroll % 2 == 0 and (nq * nkb) % unroll == 0 and nkb & (nkb - 1) == 0
    kern = functools.partial(_attn_kernel, sk=sk, mq=mq, unroll=unroll, out_scale=out_scale)
    head = lambda bb, h: (bb, 0, h)
    return pl.pallas_call(
        kern,
        grid=(b, N_HEADS),
        in_specs=[pl.BlockSpec(memory_space=pltpu.SMEM),
                  pl.BlockSpec((None, s, 128), head),
                  pl.BlockSpec((None, s, 128), head),
                  pl.BlockSpec((None, nkb, VT_ROWS, sk), lambda bb, h: (bb, 0, h, 0)),
                  pl.BlockSpec((V_DIM, 1), lambda bb, h: (0, 0))],
        out_specs=pl.BlockSpec((None, s, V_DIM), head),
        out_shape=jax.ShapeDtypeStruct((b, s, aw), BF16),
        scratch_shapes=[pltpu.VMEM((2, 2, sk, mq), F32), pltpu.VMEM((2, 2, 1, mq), F32),
                        pltpu.VMEM((nq, 2, 1, mq), F32), pltpu.VMEM((nq, 2, VT_ROWS, mq), F32)],
        compiler_params=_cparams("parallel", "parallel"),
        name="attn",
    )(lam, q, k, vt, gsub_col)


def _conv_tile(t, prev_ref, cur_ref, next_ref, w_ref, b_ref, g_ref, beta_ref, o_ref, ext_ref, sh_ref):
    tc, c = cur_ref.shape
    n_ext = tc + 2 * HALO
    keep_prev = (t > 0).astype(F32)
    keep_next = (t < pl.num_programs(1) - 1).astype(F32)
    ext_ref[0:HALO, :] = prev_ref[...] * keep_prev
    ext_ref[HALO:HALO + tc, :] = cur_ref[...]
    ext_ref[HALO + tc:n_ext, :] = next_ref[...] * keep_next
    ext_ref[n_ext:n_ext + 8, :] = jnp.zeros((8, c), F32)
    for o in range(1, 8):
        for e0 in range(0, n_ext, CONV_SHIFT_CHUNK):
            sh_ref[o - 1, e0:e0 + CONV_SHIFT_CHUNK, :] = ext_ref[e0 + o:e0 + o + CONV_SHIFT_CHUNK, :]
    w = w_ref[...]
    base = HALO - CONV_PAD
    for r0 in range(0, tc, CONV_CHUNK):
        acc = jnp.broadcast_to(b_ref[...], (CONV_CHUNK, c))
        for j in range(CONV_KERNEL):
            o = (base + j) % 8
            a0 = r0 + base + j - o
            src = ext_ref if o == 0 else sh_ref.at[o - 1]
            acc = acc + w[j:j + 1, :] * src[a0:a0 + CONV_CHUNK, :]
        mu = jnp.mean(acc, axis=-1, keepdims=True)
        cen = acc - mu
        var = jnp.mean(cen * cen, axis=-1, keepdims=True)
        y = cen * lax.rsqrt(var + EPS) * g_ref[...] + beta_ref[...]
        o_ref[r0:r0 + CONV_CHUNK, :] = (y * jax.nn.sigmoid(y)).astype(BF16)


def _mix_route_kernel(zp_ref, z_ref, zn_ref, wdw_ref, bdw_ref, gln_ref, bln_ref,
                      o_ref, x_ref, g1_ref, wo_ref, wc_ref, sh_ref, sc_ref, g2n_ref, wr_ref, br_ref, tri_ref,
                      x1_ref, h2x_ref, cnt_ref, ext_ref, shift_ref, cv_ref, carry_ref):
    first = jnp.logical_and(pl.program_id(0) == 0, pl.program_id(1) == 0)

    @pl.when(first)
    def _():
        carry_ref[...] = jnp.zeros(carry_ref.shape, F32)

    _conv_tile(pl.program_id(1), zp_ref, z_ref, zn_ref, wdw_ref, bdw_ref, gln_ref, bln_ref, cv_ref, ext_ref,
               shift_ref)
    mix = jnp.dot(o_ref[...], wo_ref[...], preferred_element_type=F32)
    mix = mix + jnp.dot(cv_ref[...], wc_ref[...], preferred_element_type=F32)
    x = x_ref[...] + g1_ref[...] * mix
    x1_ref[...] = x
    d = x.shape[1]
    ms = jnp.mean(x * x, axis=-1, keepdims=True)
    h2 = x * lax.rsqrt(ms + EPS) * g2n_ref[...] * (1.0 + sc_ref[...]) + sh_ref[...]
    h2x_ref[:, 0:d] = h2
    lg = jnp.dot(h2.astype(BF16), wr_ref[...], preferred_element_type=F32) + br_ref[...]
    lane = lax.broadcasted_iota(jnp.int32, lg.shape, 1).astype(F32)
    far = jnp.float32(1e9)
    in_g = lane < N_GROUPS
    lgm = jnp.where(in_g, lg, NEG_BIG)
    gmax = jnp.max(lgm, axis=1, keepdims=True)
    gsel = jnp.min(jnp.where(lgm == gmax, lane, far), axis=1, keepdims=True)
    p_group = 1.0 / jnp.sum(jnp.where(in_g, jnp.exp(lg - gmax), 0.0), axis=1, keepdims=True)
    lo_lane = N_GROUPS + gsel * EXPERTS_PER_GROUP
    in_e = jnp.logical_and(lane >= lo_lane, lane < lo_lane + EXPERTS_PER_GROUP)
    le = jnp.where(in_e, lg, NEG_BIG)
    v1 = jnp.max(le, axis=1, keepdims=True)
    l1 = jnp.min(jnp.where(le == v1, lane, far), axis=1, keepdims=True)
    le2 = jnp.where(lane == l1, NEG_BIG, le)
    v2 = jnp.max(le2, axis=1, keepdims=True)
    l2 = jnp.min(jnp.where(le2 == v2, lane, far), axis=1, keepdims=True)
    r = jnp.exp(v2 - v1)
    w1 = p_group / (1.0 + r)
    w2 = w1 * r
    i1 = l1 - lo_lane
    i2 = l2 - lo_lane
    a = jnp.minimum(i1, i2)
    b = jnp.maximum(i1, i2)
    w_lo = jnp.where(i1 < i2, w1, w2)
    w_hi = jnp.where(i1 < i2, w2, w1)
    cls = gsel * N_PAIRS + a * (2 * EXPERTS_PER_GROUP - 1 - a) * 0.5 + (b - a - 1.0)
    onehot = (lane == cls).astype(F32)
    excl = jnp.dot(tri_ref[...], onehot.astype(BF16), preferred_element_type=F32)
    rank = jnp.sum(onehot * (excl + carry_ref[...]), axis=1, keepdims=True)
    carry_ref[...] = carry_ref[...] + jnp.sum(onehot, axis=0, keepdims=True)
    info = jnp.where(lane == 0, cls, jnp.where(lane == 1, rank, jnp.where(lane == 2, w_lo,
                                                                           jnp.where(lane == 3, w_hi, 0.0))))
    h2x_ref[:, d:d + INFO_W] = info
    cnt_ref[...] = jnp.broadcast_to(carry_ref[...], cnt_ref.shape)


def _mix_route(z, w_dw, b_dw, g_ln, b_ln, o, x, g1, w_o, w_c, sh2, sc2, g2n, wr, br, tri):
    b, s, d = x.shape
    tr = ROUTE_TILE
    aw = o.shape[-1]
    c = z.shape[-1]
    nh = tr // HALO
    last = s // HALO - 1
    tile = lambda bb, i: (bb, i, 0)
    row = lambda bb, i: (bb, 0, 0)
    const2 = lambda bb, i: (0, 0)
    return pl.pallas_call(
        _mix_route_kernel,
        grid=(b, s // tr),
        in_specs=[pl.BlockSpec((None, HALO, c), lambda bb, i: (bb, jnp.maximum(i * nh - 1, 0), 0)),
                  pl.BlockSpec((None, tr, c), tile),
                  pl.BlockSpec((None, HALO, c), lambda bb, i: (bb, jnp.minimum((i + 1) * nh, last), 0)),
                  pl.BlockSpec((CONV_KERNEL, c), const2),
                  pl.BlockSpec((1, c), const2), pl.BlockSpec((1, c), const2), pl.BlockSpec((1, c), const2),
                  pl.BlockSpec((None, tr, aw), tile),
                  pl.BlockSpec((None, tr, d), tile), pl.BlockSpec((None, 1, d), row),
                  pl.BlockSpec(w_o.shape, const2), pl.BlockSpec(w_c.shape, const2),
                  pl.BlockSpec((None, 1, d), row), pl.BlockSpec((None, 1, d), row),
                  pl.BlockSpec((1, d), const2), pl.BlockSpec(wr.shape, const2), pl.BlockSpec((1, 128), const2),
                  pl.BlockSpec((tr, tr), const2)],
        out_specs=[pl.BlockSpec((None, tr, d), tile), pl.BlockSpec((None, tr, d + INFO_W), tile),
                   pl.BlockSpec((8, 128), const2)],
        out_shape=[jax.ShapeDtypeStruct((b, s, d), F32), jax.ShapeDtypeStruct((b, s, d + INFO_W), F32),
                   jax.ShapeDtypeStruct((8, 128), F32)],
        scratch_shapes=[pltpu.VMEM((tr + 2 * HALO + 8, c), F32), pltpu.VMEM((7, tr + 2 * HALO, c), F32),
                        pltpu.VMEM((tr, c), BF16), pltpu.VMEM((1, 128), F32)],
        compiler_params=_cparams("arbitrary", "arbitrary"),
        name="mix_route",
    )(z, z, z, w_dw, b_dw, g_ln, b_ln, o, x, g1, w_o, w_c, sh2, sc2, g2n, wr, br, tri)


def _dest_kernel(info_ref, start_ref, dest_ref):
    info_t = info_ref[...].T
    cls = info_t[0:1, :]
    rank = info_t[1:2, :]
    cid = lax.broadcasted_iota(jnp.int32, info_t.shape, 0).astype(F32)
    base = jnp.sum(jnp.where(cid == cls, start_ref[...], 0.0), axis=0, keepdims=True)
    dest_ref[...] = (base + rank).astype(jnp.int32)


def _dest(h2x, start_col):
    t, w = h2x.shape
    tr = ROW_TILE
    return pl.pallas_call(
        _dest_kernel,
        grid=(t // tr,),
        in_specs=[pl.BlockSpec((tr, INFO_W), lambda i: (i, (w - INFO_W) // INFO_W)),
                  pl.BlockSpec((INFO_W, 1), lambda i: (0, 0))],
        out_specs=pl.BlockSpec((None, 1, tr), lambda i: (i, 0, 0)),
        out_shape=jax.ShapeDtypeStruct((t // tr, 1, tr), jnp.int32),
        compiler_params=_cparams("parallel"),
        name="dest",
    )(h2x, start_col)


def _row_copy(src_ref, dst_ref, src_row, dst_row, sem):
    return pltpu.make_async_copy(src_ref.at[pl.ds(src_row, 1), :], dst_ref.at[pl.ds(dst_row, 1), :], sem)


def _start_rows(n, copy_of_row):
    def start(g, c):
        for u in range(DMA_ROWS_PER_ITER):
            copy_of_row(g * DMA_ROWS_PER_ITER + u).start(priority=u % 2)
        return c

    lax.fori_loop(0, n // DMA_ROWS_PER_ITER, start, 0)


def _scatter_rows(dest_ref, h_ref, xs_ref, sem):
    n = h_ref.shape[0]
    _start_rows(n, lambda r: _row_copy(h_ref, xs_ref, r, dest_ref[0, r], sem))

    def wait(r, c):
        _row_copy(h_ref, xs_ref, 0, 0, sem).wait()
        return c

    lax.fori_loop(0, n, wait, 0, unroll=8)


def _dispatch_kernel(dest_ref, h_ref, xs_in, xs_ref, sem):
    del xs_in
    _scatter_rows(dest_ref, h_ref, xs_ref, sem.at[0])


def _dispatch(dest3, h2x, xs):
    t, w = h2x.shape
    tr = ROW_TILE
    return pl.pallas_call(
        _dispatch_kernel,
        grid=(t // tr,),
        in_specs=[pl.BlockSpec((None, 1, tr), lambda i: (i, 0, 0), memory_space=pltpu.SMEM),
                  pl.BlockSpec((tr, w), lambda i: (i, 0)),
                  pl.BlockSpec(memory_space=pl.ANY)],
        out_specs=pl.BlockSpec(memory_space=pl.ANY),
        out_shape=jax.ShapeDtypeStruct(xs.shape, F32),
        scratch_shapes=[pltpu.SemaphoreType.DMA((1,))],
        input_output_aliases={2: 0},
        compiler_params=_cparams("arbitrary"),
        name="dispatch",
    )(dest3, h2x, xs)


def _moe_kernel(elo_ref, ehi_ref, nused_ref, xs_ref, wgu_lo, wd_lo, wgu_hi, wd_hi, ys_ref):
    del elo_ref, ehi_ref
    used = pl.program_id(0) < nused_ref[0]
    d = ys_ref.shape[1]

    @pl.when(used)
    def _():
        x = xs_ref[:, 0:d].astype(BF16)
        info = xs_ref[:, d:d + INFO_W]
        ff = wd_lo.shape[0]

        def expert(wgu, wd):
            gu = jnp.dot(x, wgu[...], preferred_element_type=F32)
            g = gu[:, :ff]
            u = gu[:, ff:]
            act = g * jax.nn.sigmoid(g) * u
            return jnp.dot(act.astype(BF16), wd[...], preferred_element_type=F32)

        ys_ref[...] = info[:, 2:3] * expert(wgu_lo, wd_lo) + info[:, 3:4] * expert(wgu_hi, wd_hi)

    @pl.when(jnp.logical_not(used))
    def _():
        ys_ref[...] = jnp.zeros(ys_ref.shape, F32)


def _moe(blk_elo, blk_ehi, n_used, xs, w_gu, w_d):
    n_rows, w = xs.shape
    d = w - INFO_W
    tb = MOE_BLOCK
    n_blk = n_rows // tb
    ff = w_d.shape[1]
    grid_spec = pltpu.PrefetchScalarGridSpec(
        num_scalar_prefetch=3,
        grid=(n_blk,),
        in_specs=[pl.BlockSpec((tb, w), lambda i, elo, ehi, nu: (jnp.minimum(i, nu[0] - 1), 0)),
                  pl.BlockSpec((None, d, 2 * ff), lambda i, elo, ehi, nu: (elo[i], 0, 0)),
                  pl.BlockSpec((None, ff, d), lambda i, elo, ehi, nu: (elo[i], 0, 0)),
                  pl.BlockSpec((None, d, 2 * ff), lambda i, elo, ehi, nu: (ehi[i], 0, 0)),
                  pl.BlockSpec((None, ff, d), lambda i, elo, ehi, nu: (ehi[i], 0, 0))],
        out_specs=pl.BlockSpec((tb, d), lambda i, elo, ehi, nu: (i, 0)),
    )
    return pl.pallas_call(
        _moe_kernel,
        grid_spec=grid_spec,
        out_shape=jax.ShapeDtypeStruct((n_rows, d), F32),
        compiler_params=_cparams("arbitrary"),
        name="moe",
    )(blk_elo, blk_ehi, n_used, xs, w_gu, w_d, w_gu, w_d)


def _combine_kernel(dest_ref, x1_ref, g2_ref, ys_ref, o_ref, buf_ref, sem):
    n = x1_ref.shape[0]
    _start_rows(n, lambda r: _row_copy(ys_ref, buf_ref, dest_ref[0, r], r, sem.at[0]))

    def wait(r, c):
        _row_copy(ys_ref, buf_ref, 0, 0, sem.at[0]).wait()
        return c

    lax.fori_loop(0, n, wait, 0, unroll=8)
    o_ref[...] = x1_ref[...] + g2_ref[...] * buf_ref[...]


def _combine(dest4, x1, g2, ys):
    b, s, d = x1.shape
    tr = ROW_TILE
    return pl.pallas_call(
        _combine_kernel,
        grid=(b, s // tr),
        in_specs=[pl.BlockSpec((None, None, 1, tr), lambda bb, i: (bb, i, 0, 0), memory_space=pltpu.SMEM),
                  pl.BlockSpec((None, tr, d), lambda bb, i: (bb, i, 0)),
                  pl.BlockSpec((None, 1, d), lambda bb, i: (bb, 0, 0)),
                  pl.BlockSpec(memory_space=pl.ANY)],
        out_specs=pl.BlockSpec((None, tr, d), lambda bb, i: (bb, i, 0)),
        out_shape=jax.ShapeDtypeStruct((b, s, d), F32),
        scratch_shapes=[pltpu.VMEM((tr, d), F32), pltpu.SemaphoreType.DMA((1,))],
        compiler_params=_cparams("arbitrary", "arbitrary"),
        name="combine",
    )(dest4, x1, g2, ys)


def _class_tables():
    elo, ehi = [], []
    for g in range(N_GROUPS):
        for a in range(EXPERTS_PER_GROUP):
            for b in range(a + 1, EXPERTS_PER_GROUP):
                elo.append(g * EXPERTS_PER_GROUP + a)
                ehi.append(g * EXPERTS_PER_GROUP + b)
    return np.asarray(elo, np.int32), np.asarray(ehi, np.int32)


def _rope_tables(s):
    inv_freq = ROPE_THETA ** (-jnp.arange(0, ROT_DIM, 2, dtype=F32) / ROT_DIM)
    ang = jnp.arange(s, dtype=F32)[:, None] * inv_freq[None, :]
    cos8, sin8 = jnp.cos(ang), jnp.sin(ang)
    ones = jnp.ones((s, HEAD_DIM - ROT_DIM), F32)
    cos64 = jnp.concatenate([cos8, cos8, ones], axis=1)
    sin64 = jnp.concatenate([-sin8, sin8, 0.0 * ones], axis=1)
    return jnp.tile(cos64, (1, 128 // HEAD_DIM)), jnp.tile(sin64, (1, 128 // HEAD_DIM))


def kernel(x_prompt, x_sample, c_prompt, c_sample, w_ada, b_ada, g_norm1, w_in, g_q, g_k, lambda_q1, lambda_k1, lambda_q2, lambda_k2, g_subln, w_dw, b_dw, g_conv_ln, b_conv_ln, w_out, g_norm2, w_router_group, b_router_group, w_router_expert, b_router_expert, w_gate_up, w_down):
    depth = w_ada.shape[0]
    d = x_prompt.shape[-1]
    qw = N_HEADS * 2 * HEAD_DIM
    aw = N_HEADS * V_DIM
    elo_tab, ehi_tab = _class_tables()
    seg = jnp.asarray(np.kron(np.eye(qw // HEAD_DIM), np.ones((HEAD_DIM, HEAD_DIM))), BF16)
    tri = jnp.asarray(np.tril(np.ones((ROUTE_TILE, ROUTE_TILE)), -1), BF16)
    groups = [x_prompt, x_sample]
    conds = [c_prompt, c_sample]
    nb = [c.shape[0] for c in conds]

    for l in range(depth):
        lam_init = 0.8 - 0.6 * math.exp(-0.3 * l)
        lam = (jnp.exp(jnp.sum(lambda_q1[l] * lambda_k1[l])) - jnp.exp(jnp.sum(lambda_q2[l] * lambda_k2[l]))
               + lam_init).reshape(1).astype(F32)
        mod = _ada(jnp.concatenate(conds, axis=0), w_ada[l], b_ada[l])
        w_in_b = w_in[l].astype(BF16)
        w_o_b = w_out[l][:aw].astype(BF16)
        w_c_b = w_out[l][aw:].astype(BF16)
        w_gu_b = w_gate_up[l].astype(BF16)
        w_d_b = w_down[l].astype(BF16)
        gq = jnp.tile(g_q[l], qw // HEAD_DIM).reshape(1, qw)
        gk = jnp.tile(g_k[l], qw // HEAD_DIM).reshape(1, qw)
        n_r = N_GROUPS + N_GROUPS * EXPERTS_PER_GROUP
        wr = jnp.zeros((d, 128), F32).at[:, :N_GROUPS].set(w_router_group[l]).at[:, N_GROUPS:n_r].set(
            w_router_expert[l]).astype(BF16)
        br = jnp.zeros((1, 128), F32).at[0, :N_GROUPS].set(b_router_group[l]).at[0, N_GROUPS:n_r].set(
            b_router_expert[l])

        x1s, h2xs, counts, g2s = [], [], [], []
        row0 = 0
        for x, n in zip(groups, nb):
            b, s, _ = x.shape
            m = mod[row0:row0 + n].reshape(n, 1, 6 * d)
            row0 += n
            sh1, sc1, g1, sh2, sc2, g2 = [m[:, :, i * d:(i + 1) * d] for i in range(6)]
            cos_t, sin_t = _rope_tables(s)
            q, k, vt, z = _proj(x, sh1, sc1, g_norm1[l].reshape(1, d), w_in_b, seg, gq, gk, cos_t, sin_t)
            o = _attention(lam, q, k, vt, g_subln[l].reshape(V_DIM, 1), 1.0 - lam_init)
            x1, h2x, cnt = _mix_route(z, w_dw[l], b_dw[l].reshape(1, -1), g_conv_ln[l].reshape(1, -1),
                                      b_conv_ln[l].reshape(1, -1), o, x, g1, w_o_b, w_c_b, sh2, sc2,
                                      g_norm2[l].reshape(1, d), wr, br, tri)
            x1s.append(x1)
            h2xs.append(h2x.reshape(b * s, d + INFO_W))
            counts.append(cnt[0, :N_CLASSES].astype(jnp.int32))
            g2s.append(g2)

        t_all = sum(h.shape[0] for h in h2xs)
        n_rows = t_all + N_CLASSES * MOE_BLOCK
        n_blk = n_rows // MOE_BLOCK
        tot = sum(counts)
        padded = ((tot + MOE_BLOCK - 1) // MOE_BLOCK) * MOE_BLOCK
        pad_end = jnp.cumsum(padded)
        pad_start = pad_end - padded
        dests, offs = [], jnp.zeros_like(tot)
        for h2x, cnt in zip(h2xs, counts):
            start_col = jnp.zeros((INFO_W, 1), F32).at[:N_CLASSES, 0].set((pad_start + offs).astype(F32))
            dests.append(_dest(h2x, start_col))
            offs = offs + cnt
        blk_cls = jnp.minimum(jnp.searchsorted(pad_end, jnp.arange(n_blk, dtype=jnp.int32) * MOE_BLOCK, side='right'),
                              N_CLASSES - 1)
        blk_elo = jnp.asarray(elo_tab)[blk_cls]
        blk_ehi = jnp.asarray(ehi_tab)[blk_cls]
        n_used = (pad_end[-1:] // MOE_BLOCK).astype(jnp.int32)

        xs = jnp.zeros((n_rows, d + INFO_W), F32)
        for h2x, dest in zip(h2xs, dests):
            xs = _dispatch(dest, h2x, xs)
        ys = _moe(blk_elo, blk_ehi, n_used, xs, w_gu_b, w_d_b)
        outs = []
        for x1, dest, g2 in zip(x1s, dests, g2s):
            b, s, _ = x1.shape
            outs.append(_combine(dest.reshape(b, s // ROW_TILE, 1, ROW_TILE), x1, g2, ys))
        groups = outs
    return tuple(groups)
```
